```python
import functools
import jax, jax.numpy as jnp
from jax import lax
import numpy as np

D_MODEL = 1024
BATCH = 16
SEQ = 2048
DEPTH = 1
DEC_BATCH = 128
DEC_SEQ = 8
PAST_LEN = 8192
PAGE_SIZE = 128

RET_HEADS = 4
RET_DK = 128
RET_DV = 256
RET_CHUNK = 128
MOBA_HEADS = 8
MOBA_DH = 64
MOBA_BLOCK = 256
MOBA_TOPK = 3
MOBA_QCHUNK = 64
D_FF = 2816
CONV_W = 3
LN_EPS = 1e-5
GN_EPS = 1e-6
DEEPNORM_ALPHA = (2 * DEPTH) ** 0.25
DEEPNORM_BETA = (8 * DEPTH) ** -0.25

RET_QK_W = RET_HEADS * RET_DK
RET_V_W = RET_HEADS * RET_DV
MOBA_W = MOBA_HEADS * MOBA_DH
IN_SPLITS = (RET_QK_W, RET_QK_W, RET_V_W, RET_V_W, MOBA_W, MOBA_W, MOBA_W, D_MODEL, D_MODEL)
IN_V_SEGMENTS = (2, 6)

kernel_name = 'retnet_moba_gated_hybrid_step'


def layer_norm(x, w, b):
    xf = x.astype(jnp.float32)
    mu = xf.mean(-1, keepdims=True)
    var = jnp.square(xf - mu).mean(-1, keepdims=True)
    return ((xf - mu) * lax.rsqrt(var + LN_EPS) * w + b).astype(x.dtype)


def retention_log_decay():
    return jnp.log1p(-jnp.exp2(-5.0 - jnp.arange(RET_HEADS, dtype=jnp.float32)))


def alibi_slopes(n_heads):
    return jnp.exp2(-8.0 * (jnp.arange(n_heads, dtype=jnp.float32) + 1.0) / n_heads)


def retention(q, k, v, s0, chunk):
    B, T, H, _ = q.shape
    DV = v.shape[-1]
    n = T // chunk
    log_g = retention_log_decay()
    i = jnp.arange(chunk, dtype=jnp.float32)
    diff = i[:, None] - i[None, :]
    intra = jnp.where(diff >= 0, jnp.exp(jnp.maximum(diff, 0.0)[None] * log_g[:, None, None]), 0.0)
    q_dec = jnp.exp((i[None] + 1.0) * log_g[:, None]).T
    k_dec = jnp.exp((chunk - 1.0 - i[None]) * log_g[:, None]).T
    c_dec = jnp.exp(chunk * log_g)

    def to_chunks(a):
        return a.astype(jnp.float32).reshape(B, n, chunk, H, a.shape[-1]).swapaxes(0, 1)

    def step(S, inp):
        qc, kc, vc = inp
        a = jnp.einsum('bihd,bjhd->bhij', qc, kc) * intra
        o = (jnp.einsum('bhij,bjhv->bihv', a, vc)
             + jnp.einsum('bihd,bhdv->bihv', qc * q_dec[None, :, :, None], S))
        S = c_dec[None, :, None, None] * S + jnp.einsum('bjhd,bjhv->bhdv', kc * k_dec[None, :, :, None], vc)
        return S, o

    S, o = lax.scan(step, s0.astype(jnp.float32), (to_chunks(q), to_chunks(k), to_chunks(v)))
    return o.swapaxes(0, 1).reshape(B, T, H, DV), S


def moba_one_seq(q, k, v, q_pos, q_chunk):
    Tq, H, Dh = q.shape
    L = k.shape[0]
    nb = -(-L // MOBA_BLOCK)
    pad = nb * MOBA_BLOCK - L
    k = jnp.pad(k, ((0, pad), (0, 0), (0, 0)))
    v = jnp.pad(v, ((0, pad), (0, 0), (0, 0)))
    kb = k.reshape(nb, MOBA_BLOCK, H, Dh).transpose(2, 0, 1, 3)
    vb = v.reshape(nb, MOBA_BLOCK, H, Dh).transpose(2, 0, 1, 3)
    k_mean = jnp.mean(kb.astype(jnp.float32), axis=2)
    n_sel = min(MOBA_TOPK, nb)
    slopes = alibi_slopes(H)
    blk_ids = jnp.arange(nb, dtype=jnp.int32)
    offs = jnp.arange(MOBA_BLOCK, dtype=jnp.int32)
    h_idx = jnp.arange(H)[None, :, None]
    scale = Dh ** -0.5

    def one_chunk(args):
        qq, pos = args
        cur = (pos // MOBA_BLOCK)[:, None, None]
        gate = jnp.einsum('qhd,hnd->qhn', qq.astype(jnp.float32), k_mean)
        gate = jnp.where(blk_ids[None, None, :] < cur, gate, -jnp.inf)
        _, sel = lax.top_k(gate, n_sel)
        own = jnp.broadcast_to(cur, sel.shape[:2] + (1,))
        idx = jnp.concatenate([sel, own], axis=-1)
        blk_ok = jnp.concatenate([sel < cur, jnp.ones(own.shape, bool)], axis=-1)
        kg = kb[h_idx, idx]
        vg = vb[h_idx, idx]
        s = jnp.einsum('qhd,qhnkd->qhnk', qq, kg).astype(jnp.float32) * scale
        dist = pos[:, None, None, None] - (idx[..., None] * MOBA_BLOCK + offs)
        s = s - slopes[None, :, None, None] * dist.astype(jnp.float32)
        s = jnp.where(blk_ok[..., None] & (dist >= 0), s, -jnp.inf)
        p = jax.nn.softmax(s, axis=(-2, -1))
        return jnp.einsum('qhnk,qhnkd->qhd', p.astype(vg.dtype), vg)

    nq = Tq // q_chunk
    o = lax.map(one_chunk, (q.reshape(nq, q_chunk, H, Dh), q_pos.reshape(nq, q_chunk)))
    return o.reshape(Tq, H, Dh)


def moba_prompt(q, k, v):
    T = q.shape[1]
    pos = jnp.arange(T, dtype=jnp.int32)
    qc = min(MOBA_QCHUNK, T)
    return lax.map(lambda a: moba_one_seq(a[0], a[1], a[2], pos, qc), (q, k, v))


def moba_sample(q, k, v, cache_k, cache_v, page_table, layer):
    T = q.shape[1]
    past = page_table.shape[1] * cache_k.shape[2]
    pos = past + jnp.arange(T, dtype=jnp.int32)

    def one(args):
        qb, kn, vn, pages = args
        kp = cache_k[layer, pages].reshape(past, MOBA_HEADS, MOBA_DH).astype(kn.dtype)
        vp = cache_v[layer, pages].reshape(past, MOBA_HEADS, MOBA_DH).astype(vn.dtype)
        return moba_one_seq(qb, jnp.concatenate([kp, kn], 0), jnp.concatenate([vp, vn], 0), pos, T)

    return lax.map(one, (q, k, v, page_table))


def causal_dwconv(u, prev, w, b):
    T = u.shape[1]
    full = jnp.concatenate([prev.astype(u.dtype), u], axis=1)
    y = b + sum(full[:, j:j + T] * w[j] for j in range(CONV_W))
    return y, full[:, -(CONV_W - 1):]


def trunk_layer(x, ret_state, conv_state, ret_chunk, attend, w_in, ret_gn_w, w_ret_out, w_moba_out,
                w_o, ln1_w, ln1_b, w_up, conv_w, conv_b, w_down, ln2_w, ln2_b):
    B, T, _ = x.shape
    offsets = np.cumsum(IN_SPLITS)[:-1].tolist()
    rq, rk, rv, rg, mq, mk, mv, ga, gb = jnp.split(x @ w_in, offsets, axis=-1)
    ro, ret_new = retention(rq.reshape(B, T, RET_HEADS, RET_DK),
                            rk.reshape(B, T, RET_HEADS, RET_DK) * RET_DK ** -0.5,
                            rv.reshape(B, T, RET_HEADS, RET_DV), ret_state, ret_chunk)
    mu = ro.mean(-1, keepdims=True)
    var = jnp.square(ro - mu).mean(-1, keepdims=True)
    ro = ((ro - mu) * lax.rsqrt(var + GN_EPS)).reshape(B, T, RET_V_W) * ret_gn_w
    ro = (jax.nn.silu(rg.astype(jnp.float32)) * ro).astype(x.dtype)
    mk = mk.reshape(B, T, MOBA_HEADS, MOBA_DH)
    mv = mv.reshape(B, T, MOBA_HEADS, MOBA_DH)
    mo = attend(mq.reshape(B, T, MOBA_HEADS, MOBA_DH), mk, mv).reshape(B, T, MOBA_W)
    merged = jax.nn.sigmoid(ga) * (ro @ w_ret_out) + jax.nn.sigmoid(gb) * (mo @ w_moba_out)
    h = layer_norm(DEEPNORM_ALPHA * x + merged @ w_o, ln1_w, ln1_b)
    ua, ug = jnp.split(h @ w_up, [D_FF], axis=-1)
    uc, conv_new = causal_dwconv(ua, conv_state, conv_w, conv_b)
    f = (jax.nn.silu(uc) * ug) @ w_down
    y = layer_norm(DEEPNORM_ALPHA * h + f, ln2_w, ln2_b)
    return y, ret_new, conv_new, mk, mv


def setup_inputs(seed: int = 0) -> dict:
    key = jax.random.key(seed)
    ks = jax.random.split(key, 24)
    f32 = jnp.float32
    n_pages = PAST_LEN // PAGE_SIZE
    n_used = DEC_BATCH * n_pages
    n_phys = n_used + max(1, n_used // 4)
    page_table = jax.random.permutation(ks[0], n_phys)[:n_used].reshape(DEC_BATCH, n_pages).astype(jnp.int32)
    col_scale = jnp.concatenate([jnp.full((w,), DEEPNORM_BETA if i in IN_V_SEGMENTS else 1.0, f32)
                                 for i, w in enumerate(IN_SPLITS)])
    n_in = sum(IN_SPLITS)
    nrm = lambda k, shp: jax.random.normal(k, shp, f32)
    return {
        'x_prompt': nrm(ks[1], (BATCH, SEQ, D_MODEL)),
        'x_sample': nrm(ks[2], (DEC_BATCH, DEC_SEQ, D_MODEL)),
        'cache_k': nrm(ks[3], (DEPTH, n_phys, PAGE_SIZE, MOBA_HEADS, MOBA_DH)),
        'cache_v': nrm(ks[4], (DEPTH, n_phys, PAGE_SIZE, MOBA_HEADS, MOBA_DH)) * DEEPNORM_BETA,
        'page_table': page_table,
        'state_ret': nrm(ks[5], (DEPTH, DEC_BATCH, RET_HEADS, RET_DK, RET_DV)),
        'state_conv': nrm(ks[6], (DEPTH, DEC_BATCH, CONV_W - 1, D_FF)) * DEEPNORM_BETA,
        'w_in': nrm(ks[7], (DEPTH, D_MODEL, n_in)) * D_MODEL ** -0.5 * col_scale,
        'ret_gn_w': 1.0 + 0.02 * nrm(ks[8], (DEPTH, RET_V_W)),
        'w_ret_out': nrm(ks[9], (DEPTH, RET_V_W, D_MODEL)) * RET_V_W ** -0.5 * DEEPNORM_BETA,
        'w_moba_out': nrm(ks[10], (DEPTH, MOBA_W, D_MODEL)) * MOBA_W ** -0.5 * DEEPNORM_BETA,
        'w_o': nrm(ks[11], (DEPTH, D_MODEL, D_MODEL)) * D_MODEL ** -0.5 * DEEPNORM_BETA,
        'ln1_w': 1.0 + 0.02 * nrm(ks[12], (DEPTH, D_MODEL)),
        'ln1_b': 0.02 * nrm(ks[13], (DEPTH, D_MODEL)),
        'w_up': nrm(ks[14], (DEPTH, D_MODEL, 2 * D_FF)) * D_MODEL ** -0.5 * DEEPNORM_BETA,
        'conv_w': nrm(ks[15], (DEPTH, CONV_W, D_FF)) * CONV_W ** -0.5,
        'conv_b': 0.02 * nrm(ks[16], (DEPTH, D_FF)),
        'w_down': nrm(ks[17], (DEPTH, D_FF, D_MODEL)) * D_FF ** -0.5 * DEEPNORM_BETA,
        'ln2_w': 1.0 + 0.02 * nrm(ks[18], (DEPTH, D_MODEL)),
        'ln2_b': 0.02 * nrm(ks[19], (DEPTH, D_MODEL)),
    }


def reference(x_prompt, x_sample, cache_k, cache_v, page_table, state_ret, state_conv, w_in, ret_gn_w,
              w_ret_out, w_moba_out, w_o, ln1_w, ln1_b, w_up, conv_w, conv_b, w_down, ln2_w, ln2_b):
    B = x_prompt.shape[0]
    hp, hs = x_prompt, x_sample
    kp_l, vp_l, rp_l, cp_l, ks_l, vs_l, rs_l, cs_l = [], [], [], [], [], [], [], []
    for l in range(DEPTH):
        wts = (w_in[l], ret_gn_w[l], w_ret_out[l], w_moba_out[l], w_o[l], ln1_w[l], ln1_b[l],
               w_up[l], conv_w[l], conv_b[l], w_down[l], ln2_w[l], ln2_b[l])
        hp, rp, cp, kp, vp = trunk_layer(
            hp, jnp.zeros((B, RET_HEADS, RET_DK, RET_DV), jnp.float32),
            jnp.zeros((B, CONV_W - 1, D_FF), hp.dtype), RET_CHUNK, moba_prompt, *wts)
        attend_s = functools.partial(moba_sample, cache_k=cache_k, cache_v=cache_v,
                                     page_table=page_table, layer=l)
        hs, rs, cs, ks_, vs_ = trunk_layer(hs, state_ret[l], state_conv[l], hs.shape[1], attend_s, *wts)
        kp_l.append(kp); vp_l.append(vp); rp_l.append(rp); cp_l.append(cp)
        ks_l.append(ks_); vs_l.append(vs_); rs_l.append(rs); cs_l.append(cs)
    return (hp, hs, jnp.stack(kp_l), jnp.stack(vp_l), jnp.stack(rp_l), jnp.stack(cp_l),
            jnp.stack(ks_l), jnp.stack(vs_l), jnp.stack(rs_l), jnp.stack(cs_l))
```

```python
import functools

import jax
import jax.numpy as jnp
from jax import lax
from jax.experimental import pallas as pl
from jax.experimental.pallas import tpu as pltpu

D_MODEL = 1024
RET_HEADS = 4
RET_DK = 128
RET_DV = 256
RET_CHUNK = 128
MOBA_HEADS = 8
MOBA_DH = 64
MOBA_BLOCK = 256
MOBA_TOPK = 3
D_FF = 2816
CONV_W = 3
LN_EPS = 1e-5
GN_EPS = 1e-6

RET_QK_W = RET_HEADS * RET_DK
RET_V_W = RET_HEADS * RET_DV
MOBA_W = MOBA_HEADS * MOBA_DH
IN_SPLITS = (RET_QK_W, RET_QK_W, RET_V_W, RET_V_W, MOBA_W, MOBA_W, MOBA_W, D_MODEL, D_MODEL)
N_IN = sum(IN_SPLITS)

LANES = 128
VMEM_LIMIT_BYTES = 56 * 1024 * 1024
NEG_BIG = -1e30

_BF = jnp.bfloat16
_F32 = jnp.float32


def _nt(a, b):
    return lax.dot_general(a, b, (((1,), (1,)), ((), ())), preferred_element_type=_F32)


def _nn(a, b):
    return jnp.dot(a, b, preferred_element_type=_F32)


def _split_bf16(x):
    hi = x.astype(_BF)
    lo = (x - hi.astype(_F32)).astype(_BF)
    return hi, lo


def _nt3(a, b):
    ah, al = _split_bf16(a)
    bh, bl = _split_bf16(b)
    return _nt(ah, bh) + (_nt(al, bh) + _nt(ah, bl))


def _layer_norm(x, w, b):
    mu = jnp.mean(x, axis=-1, keepdims=True)
    d = x - mu
    var = jnp.mean(d * d, axis=-1, keepdims=True)
    return d * lax.rsqrt(var + LN_EPS) * w + b


def _const_spec(shape):
    return pl.BlockSpec(shape, lambda *_: (0,) * len(shape), pipeline_mode=pl.Buffered(1))


def _params(n_grid):
    return pltpu.CompilerParams(dimension_semantics=("arbitrary",) * n_grid,
                                vmem_limit_bytes=VMEM_LIMIT_BYTES)


def _in_proj_kernel(x_ref, w_ref, rq_ref, rk_ref, rv_ref, rg_ref, mq_ref, mk_ref, mv_ref, gab_ref):
    xb = x_ref[...].astype(_BF)
    offs = [0]
    for w in IN_SPLITS:
        offs.append(offs[-1] + w)

    def seg(i, j=None):
        return _nn(xb, w_ref[:, offs[i]:offs[(i if j is None else j) + 1]])

    rq_ref[...] = seg(0).astype(rq_ref.dtype)
    rk_ref[...] = (seg(1) * RET_DK ** -0.5).astype(rk_ref.dtype)
    rv_ref[...] = seg(2).astype(rv_ref.dtype)
    rg_ref[...] = seg(3).astype(rg_ref.dtype)
    mq_ref[...] = seg(4)
    mk_ref[...] = seg(5)
    mv_ref[...] = seg(6)
    gab_ref[...] = seg(7, 8).astype(gab_ref.dtype)


def _in_proj(x2d, w_in_bf, inter_dtype, tm):
    n = x2d.shape[0]
    assert n % tm == 0
    widths = (RET_QK_W, RET_QK_W, RET_V_W, RET_V_W, MOBA_W, MOBA_W, MOBA_W, 2 * D_MODEL)
    dtypes = (inter_dtype,) * 4 + (_F32,) * 3 + (inter_dtype,)
    return pl.pallas_call(
        _in_proj_kernel,
        grid=(n // tm,),
        in_specs=[pl.BlockSpec((tm, D_MODEL), lambda i: (i, 0)), _const_spec((D_MODEL, N_IN))],
        out_specs=[pl.BlockSpec((tm, w), lambda i: (i, 0)) for w in widths],
        out_shape=[jax.ShapeDtypeStruct((n, w), dt) for w, dt in zip(widths, dtypes)],
        compiler_params=_params(1),
        name="in_proj",
    )(x2d, w_in_bf)


def _retention_kernel(q_ref, k_ref, v_ref, g_ref, s0_ref, intra_ref, qdec_ref, kdec_ref, cdec_ref,
                      gnw_ref, ro_ref, sn_ref, s_scr, *, chunk, cpad, n_chunks):
    s_scr[...] = s0_ref[0, 0]
    intra = intra_ref[0]
    qdec = qdec_ref[0]
    kdec = kdec_ref[0]
    cdec = cdec_ref[0]
    gnw = gnw_ref[...]

    def pad(a):
        if chunk == cpad:
            return a
        return jnp.concatenate([a, jnp.zeros((cpad - chunk, a.shape[1]), a.dtype)], axis=0)

    def body(c, carry):
        rows = pl.ds(pl.multiple_of(c * chunk, chunk), chunk)
        q = pad(q_ref[rows, :].astype(_F32))
        k = pad(k_ref[rows, :].astype(_F32))
        v = pad(v_ref[rows, :].astype(_F32)).astype(_BF)
        a = _nt(q.astype(_BF), k.astype(_BF)) * intra
        s = s_scr[...]
        o = _nn(a.astype(_BF), v) + _nn((q * qdec).astype(_BF), s.astype(_BF))
        s_scr[...] = cdec * s + _nn((k * kdec).T.astype(_BF), v)
        o = o[:chunk]
        mu = jnp.mean(o, axis=-1, keepdims=True)
        d = o - mu
        var = jnp.mean(d * d, axis=-1, keepdims=True)
        on = d * lax.rsqrt(var + GN_EPS) * gnw
        g = g_ref[rows, :].astype(_F32)
        ro_ref[rows, :] = (g * jax.nn.sigmoid(g) * on).astype(ro_ref.dtype)
        return carry

    lax.fori_loop(0, n_chunks, body, 0)
    sn_ref[0, 0] = s_scr[...]


def _decay_tables(chunk, cpad):
    log_g = jnp.log1p(-jnp.exp2(-5.0 - jnp.arange(RET_HEADS, dtype=_F32)))
    i = jnp.arange(chunk, dtype=_F32)
    diff = i[:, None] - i[None, :]
    intra = jnp.where(diff >= 0, jnp.exp(jnp.maximum(diff, 0.0)[None] * log_g[:, None, None]), 0.0)
    q_dec = jnp.exp((i[None] + 1.0) * log_g[:, None])
    k_dec = jnp.exp((chunk - 1.0 - i[None]) * log_g[:, None])
    c_dec = jnp.exp(chunk * log_g)
    p = cpad - chunk
    intra = jnp.pad(intra, ((0, 0), (0, p), (0, p)))
    q_dec = jnp.broadcast_to(jnp.pad(q_dec, ((0, 0), (0, p)))[:, :, None], (RET_HEADS, cpad, RET_DK))
    k_dec = jnp.broadcast_to(jnp.pad(k_dec, ((0, 0), (0, p)))[:, :, None], (RET_HEADS, cpad, RET_DK))
    c_dec = jnp.broadcast_to(c_dec[:, None, None], (RET_HEADS, 1, RET_DV))
    return intra, q_dec, k_dec, c_dec


def _retention(rq, rk, rv, rg, s0, gn_w, batch, seq, chunk, out_dtype):
    cpad = max(chunk, LANES)
    n_chunks = seq // chunk
    assert seq % chunk == 0
    intra, q_dec, k_dec, c_dec = _decay_tables(chunk, cpad)
    kern = functools.partial(_retention_kernel, chunk=chunk, cpad=cpad, n_chunks=n_chunks)
    head_tab = lambda shape: pl.BlockSpec((1,) + shape, lambda b, h: (h, 0, 0))
    return pl.pallas_call(
        kern,
        grid=(batch, RET_HEADS),
        in_specs=[
            pl.BlockSpec((seq, RET_DK), lambda b, h: (b, h)),
            pl.BlockSpec((seq, RET_DK), lambda b, h: (b, h)),
            pl.BlockSpec((seq, RET_DV), lambda b, h: (b, h)),
            pl.BlockSpec((seq, RET_DV), lambda b, h: (b, h)),
            pl.BlockSpec((1, 1, RET_DK, RET_DV), lambda b, h: (b, h, 0, 0)),
            head_tab((cpad, cpad)), head_tab((cpad, RET_DK)), head_tab((cpad, RET_DK)),
            head_tab((1, RET_DV)),
            pl.BlockSpec((1, RET_DV), lambda b, h: (0, h)),
        ],
        out_specs=[
            pl.BlockSpec((seq, RET_DV), lambda b, h: (b, h)),
            pl.BlockSpec((1, 1, RET_DK, RET_DV), lambda b, h: (b, h, 0, 0)),
        ],
        out_shape=[
            jax.ShapeDtypeStruct((batch * seq, RET_V_W), out_dtype),
            jax.ShapeDtypeStruct((batch, RET_HEADS, RET_DK, RET_DV), _F32),
        ],
        scratch_shapes=[pltpu.VMEM((RET_DK, RET_DV), _F32)],
        compiler_params=_params(2),
        name="retention",
    )(rq, rk, rv, rg, s0, intra, q_dec, k_dec, c_dec, gn_w)


def _moba_prompt_kernel(q_ref, k_ref, v_ref, slope_ref, o_ref, *, seq, heads_per_step):
    nb = seq // MOBA_BLOCK
    blk = MOBA_BLOCK
    scale = MOBA_DH ** -0.5
    row = lax.broadcasted_iota(jnp.int32, (blk, blk), 0)
    col = lax.broadcasted_iota(jnp.int32, (blk, blk), 1)
    rel = (row - col).astype(_F32)
    causal = row >= col

    for j in range(heads_per_step):
        cs = slice(j * MOBA_DH, (j + 1) * MOBA_DH)
        qh = q_ref[:, cs]
        kh = k_ref[:, cs]
        slope = slope_ref[0:1, j * MOBA_DH:j * MOBA_DH + 1]
        kmean = jnp.concatenate(
            [jnp.sum(kh[n * blk:(n + 1) * blk], axis=0, keepdims=True) for n in range(nb)], axis=0) * (1.0 / blk)
        gate = _nt3(qh, kmean)
        qb = qh.astype(_BF)
        kb = kh.astype(_BF)
        vb = v_ref[:, cs].astype(_BF)
        for i in range(nb):
            rs = slice(i * blk, (i + 1) * blk)
            g = gate[rs]
            parts = []
            for n in range(i + 1):
                s = _nt(qb[rs], kb[n * blk:(n + 1) * blk]) * scale - slope * (rel + float((i - n) * blk))
                if n == i:
                    ok = causal
                elif i <= MOBA_TOPK:
                    ok = None
                else:
                    gn = g[:, n:n + 1]
                    rank = jnp.zeros((blk, 1), jnp.int32)
                    for m in range(i):
                        if m == n:
                            continue
                        gm = g[:, m:m + 1]
                        ahead = (gm > gn) | ((gm == gn) if m < n else False)
                        rank = rank + ahead.astype(jnp.int32)
                    ok = rank < MOBA_TOPK
                parts.append(s if ok is None else jnp.where(ok, s, NEG_BIG))
            s_all = parts[0] if len(parts) == 1 else jnp.concatenate(parts, axis=1)
            mx = jnp.max(s_all, axis=-1, keepdims=True)
            p = jnp.exp(s_all - mx)
            den = jnp.sum(p, axis=-1, keepdims=True)
            o = _nn(p.astype(_BF), vb[:(i + 1) * blk]) / den
            o_ref[rs, cs] = o.astype(o_ref.dtype)


def _moba_prompt(mq, mk, mv, slope_row, batch, seq, out_dtype):
    hps = LANES // MOBA_DH
    width = hps * MOBA_DH
    kern = functools.partial(_moba_prompt_kernel, seq=seq, heads_per_step=hps)
    spec = pl.BlockSpec((seq, width), lambda b, h: (b, h))
    return pl.pallas_call(
        kern,
        grid=(batch, MOBA_HEADS // hps),
        in_specs=[spec, spec, spec, pl.BlockSpec((1, width), lambda b, h: (0, h))],
        out_specs=spec,
        out_shape=jax.ShapeDtypeStruct((batch * seq, MOBA_W), out_dtype),
        compiler_params=_params(2),
        name="moba_prompt",
    )(mq, mk, mv, slope_row)


def _moba_sample_kernel(pt_ref, q_ref, kn_ref, vn_ref, slope_ref, ck_hbm, cv_hbm, o_ref,
                        kbuf, vbuf, s_scr, p_scr, ksum_scr, sem, *, n_pages, page, t_new, n_groups):
    b = pl.program_id(0)
    past = n_pages * page
    ppb = MOBA_BLOCK // page
    nb = n_pages // ppb
    rows = MOBA_HEADS * t_new
    scale = MOBA_DH ** -0.5
    pg = n_pages // n_groups

    def k_copy(p):
        return pltpu.make_async_copy(ck_hbm.at[pt_ref[b, p]], kbuf.at[p], sem.at[0, p // pg])

    def v_copy(p):
        return pltpu.make_async_copy(cv_hbm.at[pt_ref[b, p]], vbuf.at[p], sem.at[1, p // pg])

    for p in range(n_pages):
        k_copy(p).start()
    for p in range(n_pages):
        v_copy(p).start()

    q = q_ref[...]
    q_rows = jnp.concatenate([q] * MOBA_HEADS, axis=0)
    r_head = lax.broadcasted_iota(jnp.int32, (rows, MOBA_W), 0) // t_new
    c_head = lax.broadcasted_iota(jnp.int32, (rows, MOBA_W), 1) // MOBA_DH
    q_exp = jnp.where(r_head == c_head, q_rows, 0.0)
    q_bf = q_exp.astype(_BF)

    slope_cols = jnp.where(r_head == c_head, jnp.broadcast_to(slope_ref[...], (rows, MOBA_W)), 0.0)
    slope = jnp.sum(slope_cols, axis=-1, keepdims=True) * (1.0 / MOBA_DH)
    r_tok = lax.broadcasted_iota(jnp.int32, (rows, page), 0) % t_new
    lane = lax.broadcasted_iota(jnp.int32, (rows, page), 1)

    for grp in range(n_groups):
        for p in range(grp * pg, (grp + 1) * pg):
            k_copy(p).wait()

        def blk_body(n, carry):
            acc = jnp.zeros((1, MOBA_W), _F32)
            for u in range(ppb):
                kp = kbuf[n * ppb + u]
                acc = acc + jnp.sum(kp, axis=0, keepdims=True)
                s_scr[n * ppb + u] = _nt(q_bf, kp.astype(_BF))
            ksum_scr[pl.ds(n, 1), :] = acc
            return carry

        lax.fori_loop(grp * pg // ppb, (grp + 1) * pg // ppb, blk_body, 0)

    gate = _nt3(q_exp, ksum_scr[...] * (1.0 / MOBA_BLOCK))
    bid = lax.broadcasted_iota(jnp.int32, (rows, nb), 1)
    sel = jnp.zeros((rows, nb), jnp.bool_)
    for _ in range(min(MOBA_TOPK, nb)):
        mx = jnp.max(gate, axis=-1, keepdims=True)
        first = jnp.min(jnp.where(gate == mx, bid, nb), axis=-1, keepdims=True)
        hit = bid == first
        sel = sel | hit
        gate = jnp.where(hit, -jnp.inf, gate)

    kn = jnp.concatenate([kn_ref[...], jnp.zeros((page - t_new, MOBA_W), _F32)], axis=0)
    vn = jnp.concatenate([vn_ref[...], jnp.zeros((page - t_new, MOBA_W), _F32)], axis=0)
    s_own = _nt(q_bf, kn.astype(_BF)) * scale - slope * (r_tok - lane).astype(_F32)
    s_own = jnp.where(lane <= r_tok, s_own, NEG_BIG)
    mx = jnp.max(s_own, axis=-1, keepdims=True)

    for n in range(nb):
        ok = sel[:, n:n + 1]
        for u in range(ppb):
            p = n * ppb + u
            dist = (past - p * page) + (r_tok - lane)
            s = jnp.where(ok, s_scr[p] * scale - slope * dist.astype(_F32), NEG_BIG)
            s_scr[p] = s
            mx = jnp.maximum(mx, jnp.max(s, axis=-1, keepdims=True))

    p_own = jnp.exp(s_own - mx)
    den = jnp.sum(p_own, axis=-1, keepdims=True)

    def exp_body(p, den):
        e = jnp.exp(s_scr[p] - mx)
        p_scr[p] = e.astype(_BF)
        return den + jnp.sum(e, axis=-1, keepdims=True)

    den = lax.fori_loop(0, n_pages, exp_body, den)

    acc = _nn(p_own.astype(_BF), vn.astype(_BF))
    for grp in range(n_groups):
        for p in range(grp * pg, (grp + 1) * pg):
            v_copy(p).wait()

        def pv_body(p, acc):
            return acc + _nn(p_scr[p], vbuf[p].astype(_BF))

        acc = lax.fori_loop(grp * pg, (grp + 1) * pg, pv_body, acc)

    out = acc / den
    o_ref[...] = jnp.concatenate(
        [out[h * t_new:(h + 1) * t_new, h * MOBA_DH:(h + 1) * MOBA_DH] for h in range(MOBA_HEADS)],
        axis=1).astype(o_ref.dtype)


def _moba_sample(mq, mk, mv, slope_row, cache_k, cache_v, page_table, t_new, out_dtype):
    batch, n_pages = page_table.shape
    n_phys, page = cache_k.shape[0], cache_k.shape[1]
    assert MOBA_BLOCK % page == 0 and n_pages % (MOBA_BLOCK // page) == 0 and t_new <= page
    assert page == LANES
    n_groups = 4 if n_pages % 4 == 0 and (n_pages // 4) % (MOBA_BLOCK // page) == 0 else 1
    rows = MOBA_HEADS * t_new
    kern = functools.partial(_moba_sample_kernel, n_pages=n_pages, page=page, t_new=t_new, n_groups=n_groups)
    tok = pl.BlockSpec((t_new, MOBA_W), lambda b, pt: (b, 0))
    grid_spec = pltpu.PrefetchScalarGridSpec(
        num_scalar_prefetch=1,
        grid=(batch,),
        in_specs=[tok, tok, tok, pl.BlockSpec((1, MOBA_W), lambda b, pt: (0, 0)),
                  pl.BlockSpec(memory_space=pl.ANY), pl.BlockSpec(memory_space=pl.ANY)],
        out_specs=tok,
        scratch_shapes=[
            pltpu.VMEM((n_pages, page, MOBA_W), _F32),
            pltpu.VMEM((n_pages, page, MOBA_W), _F32),
            pltpu.VMEM((n_pages, rows, page), _F32),
            pltpu.VMEM((n_pages, rows, page), _BF),
            pltpu.VMEM((n_pages * page // MOBA_BLOCK, MOBA_W), _F32),
            pltpu.SemaphoreType.DMA((2, n_groups)),
        ],
    )
    return pl.pallas_call(
        kern,
        grid_spec=grid_spec,
        out_shape=jax.ShapeDtypeStruct((batch * t_new, MOBA_W), out_dtype),
        compiler_params=_params(1),
        name="moba_sample",
    )(page_table, mq, mk, mv, slope_row, cache_k, cache_v)


def _merge_kernel(x_ref, ro_ref, mo_ref, gab_ref, wr_ref, wm_ref, wo_ref, lnw_ref, lnb_ref, h_ref, *, alpha):
    a = _nn(ro_ref[...].astype(_BF), wr_ref[...])
    m = _nn(mo_ref[...].astype(_BF), wm_ref[...])
    ga = gab_ref[:, :D_MODEL].astype(_F32)
    gb = gab_ref[:, D_MODEL:].astype(_F32)
    merged = jax.nn.sigmoid(ga) * a + jax.nn.sigmoid(gb) * m
    pre = alpha * x_ref[...] + _nn(merged.astype(_BF), wo_ref[...])
    h_ref[...] = _layer_norm(pre, lnw_ref[...], lnb_ref[...])


def _merge(x2d, ro, mo, gab, w_ret_out, w_moba_out, w_o, ln_w, ln_b, alpha, tm):
    n = x2d.shape[0]
    assert n % tm == 0
    row = lambda w: pl.BlockSpec((tm, w), lambda i: (i, 0))
    return pl.pallas_call(
        functools.partial(_merge_kernel, alpha=alpha),
        grid=(n // tm,),
        in_specs=[row(D_MODEL), row(RET_V_W), row(MOBA_W), row(2 * D_MODEL),
                  _const_spec((RET_V_W, D_MODEL)), _const_spec((MOBA_W, D_MODEL)),
                  _const_spec((D_MODEL, D_MODEL)), _const_spec((1, D_MODEL)), _const_spec((1, D_MODEL))],
        out_specs=row(D_MODEL),
        out_shape=jax.ShapeDtypeStruct((n, D_MODEL), _F32),
        compiler_params=_params(1),
        name="merge",
    )(x2d, ro, mo, gab, w_ret_out, w_moba_out, w_o, ln_w, ln_b)


FFN_COL_CHUNK = D_FF // 2
CONV_PAD = 8


def _ffn_kernel(*refs, alpha, tm, seq, within_seq):
    if within_seq:
        (h_ref, wup_ref, cw_ref, cb_ref, wdn_ref, lnw_ref, lnb_ref, y_ref, cs_ref, ubuf) = refs
    else:
        (h_ref, p1_ref, p2_ref, wup_ref, cw_ref, cb_ref, wdn_ref, lnw_ref, lnb_ref, y_ref, ua_ref, ubuf) = refs
    h = h_ref[...]
    hb = h.astype(_BF)
    hist = CONV_W - 1

    if within_seq:
        tiles_per_seq = seq // tm

        @pl.when(pl.program_id(0) % tiles_per_seq == 0)
        def _():
            ubuf[0:CONV_PAD, :] = jnp.zeros((CONV_PAD, D_FF), _F32)
    else:
        t_in_seq = lax.broadcasted_iota(jnp.int32, (tm, 1), 0) % seq

        @pl.when(pl.program_id(0) == 0)
        def _():
            ubuf[0:CONV_PAD, :] = jnp.zeros((CONV_PAD, D_FF), _F32)

    f = jnp.zeros((tm, D_MODEL), _F32)
    for c0 in range(0, D_FF, FFN_COL_CHUNK):
        cols = slice(c0, c0 + FFN_COL_CHUNK)
        ua = _nn(hb, wup_ref[:, c0:c0 + FFN_COL_CHUNK])
        ug = _nn(hb, wup_ref[:, D_FF + c0:D_FF + c0 + FFN_COL_CHUNK])
        ubuf[CONV_PAD:CONV_PAD + tm, cols] = ua
        sh1 = ubuf[CONV_PAD - 1:CONV_PAD - 1 + tm, cols]
        sh2 = ubuf[CONV_PAD - 2:CONV_PAD - 2 + tm, cols]
        if not within_seq:
            sh1 = jnp.where(t_in_seq >= 1, sh1, p1_ref[:, cols])
            sh2 = jnp.where(t_in_seq >= 2, sh2, p2_ref[:, cols])
            ua_ref[:, cols] = ua
        uc = cb_ref[:, cols] + (sh2 * cw_ref[0:1, cols] + sh1 * cw_ref[1:2, cols] + ua * cw_ref[2:3, cols])
        act = (uc * jax.nn.sigmoid(uc) * ug).astype(_BF)
        f = f + _nn(act, wdn_ref[c0:c0 + FFN_COL_CHUNK, :])

    if within_seq:
        last = ubuf[CONV_PAD + tm - hist:CONV_PAD + tm, :]
        cs_ref[0] = last
        ubuf[CONV_PAD - hist:CONV_PAD, :] = last
    y_ref[...] = _layer_norm(alpha * h + f, lnw_ref[...], lnb_ref[...])


def _ffn_prompt(h2d, w_up, conv_w, conv_b, w_down, ln_w, ln_b, alpha, batch, seq, tm):
    n = h2d.shape[0]
    assert seq % tm == 0 and tm >= CONV_W - 1
    tiles_per_seq = seq // tm
    row = pl.BlockSpec((tm, D_MODEL), lambda i: (i, 0))
    return pl.pallas_call(
        functools.partial(_ffn_kernel, alpha=alpha, tm=tm, seq=seq, within_seq=True),
        grid=(n // tm,),
        in_specs=[row, _const_spec((D_MODEL, 2 * D_FF)), _const_spec((CONV_W, D_FF)), _const_spec((1, D_FF)),
                  _const_spec((D_FF, D_MODEL)), _const_spec((1, D_MODEL)), _const_spec((1, D_MODEL))],
        out_specs=[row, pl.BlockSpec((1, CONV_W - 1, D_FF), lambda i: (i // tiles_per_seq, 0, 0))],
        out_shape=[jax.ShapeDtypeStruct((n, D_MODEL), _F32),
                   jax.ShapeDtypeStruct((batch, CONV_W - 1, D_FF), _F32)],
        scratch_shapes=[pltpu.VMEM((CONV_PAD + tm, D_FF), _F32)],
        compiler_params=_params(1),
        name="ffn_prompt",
    )(h2d, w_up, conv_w, conv_b, w_down, ln_w, ln_b)


def _ffn_sample(h2d, prev1, prev2, w_up, conv_w, conv_b, w_down, ln_w, ln_b, alpha, seq, tm):
    n = h2d.shape[0]
    assert n % tm == 0 and tm % seq == 0
    row = pl.BlockSpec((tm, D_MODEL), lambda i: (i, 0))
    wide = pl.BlockSpec((tm, D_FF), lambda i: (i, 0))
    return pl.pallas_call(
        functools.partial(_ffn_kernel, alpha=alpha, tm=tm, seq=seq, within_seq=False),
        grid=(n // tm,),
        in_specs=[row, wide, wide, _const_spec((D_MODEL, 2 * D_FF)), _const_spec((CONV_W, D_FF)),
                  _const_spec((1, D_FF)), _const_spec((D_FF, D_MODEL)), _const_spec((1, D_MODEL)),
                  _const_spec((1, D_MODEL))],
        out_specs=[row, wide],
        out_shape=[jax.ShapeDtypeStruct((n, D_MODEL), _F32), jax.ShapeDtypeStruct((n, D_FF), _F32)],
        scratch_shapes=[pltpu.VMEM((CONV_PAD + tm, D_FF), _F32)],
        compiler_params=_params(1),
        name="ffn_sample",
    )(h2d, prev1, prev2, w_up, conv_w, conv_b, w_down, ln_w, ln_b)


def _pick_tile(n, cap):
    t = cap
    while n % t:
        t //= 2
    return t


def _layer(x, ret_state, conv_state, ret_chunk, attend, wts, alpha, inter_dtype, row_tile):
    (w_in, gn_w, w_ret_out, w_moba_out, w_o, ln1_w, ln1_b, w_up, conv_w, conv_b, w_down, ln2_w, ln2_b) = wts
    batch, seq, _ = x.shape
    n = batch * seq
    x2d = x.reshape(n, D_MODEL)
    tm = _pick_tile(n, row_tile)
    rq, rk, rv, rg, mq, mk, mv, gab = _in_proj(x2d, w_in, inter_dtype, tm)
    ro, ret_new = _retention(rq, rk, rv, rg, ret_state, gn_w, batch, seq, ret_chunk, inter_dtype)
    mo = attend(mq, mk, mv)
    h = _merge(x2d, ro, mo, gab, w_ret_out, w_moba_out, w_o, ln1_w, ln1_b, alpha, tm)
    if conv_state is None:
        y, conv_new = _ffn_prompt(h, w_up, conv_w, conv_b, w_down, ln2_w, ln2_b, alpha, batch, seq,
                                  _pick_tile(seq, row_tile))
    else:
        hist = CONV_W - 1
        prev1 = jnp.zeros((batch, seq, D_FF), _F32).at[:, 0].set(conv_state[:, hist - 1])
        prev2 = jnp.zeros((batch, seq, D_FF), _F32).at[:, 0].set(conv_state[:, hist - 2])
        prev2 = prev2.at[:, 1].set(conv_state[:, hist - 1])
        y, ua = _ffn_sample(h, prev1.reshape(n, D_FF), prev2.reshape(n, D_FF), w_up, conv_w, conv_b, w_down,
                            ln2_w, ln2_b, alpha, seq, tm)
        conv_new = jnp.concatenate([conv_state, ua.reshape(batch, seq, D_FF)], axis=1)[:, -hist:]
    shp = (batch, seq, MOBA_HEADS, MOBA_DH)
    return y.reshape(batch, seq, D_MODEL), ret_new, conv_new, mk.reshape(shp), mv.reshape(shp)


def kernel(x_prompt, x_sample, cache_k, cache_v, page_table, state_ret, state_conv, w_in, ret_gn_w, w_ret_out,
           w_moba_out, w_o, ln1_w, ln1_b, w_up, conv_w, conv_b, w_down, ln2_w, ln2_b):
    depth = w_in.shape[0]
    alpha = (2 * depth) ** 0.25
    batch, seq, _ = x_prompt.shape
    dec_batch, dec_seq, _ = x_sample.shape
    n_phys, page = cache_k.shape[1], cache_k.shape[2]
    slope_row = jnp.repeat(jnp.exp2(-8.0 * (jnp.arange(MOBA_HEADS, dtype=_F32) + 1.0) / MOBA_HEADS),
                           MOBA_DH)[None, :]
    hp, hs = x_prompt, x_sample
    outs = [[] for _ in range(8)]
    for l in range(depth):
        wts = (w_in[l].astype(_BF), ret_gn_w[l][None, :], w_ret_out[l].astype(_BF), w_moba_out[l].astype(_BF),
               w_o[l].astype(_BF), ln1_w[l][None, :], ln1_b[l][None, :], w_up[l].astype(_BF), conv_w[l],
               conv_b[l][None, :], w_down[l].astype(_BF), ln2_w[l][None, :], ln2_b[l][None, :])
        attend_p = functools.partial(_moba_prompt, slope_row=slope_row, batch=batch, seq=seq, out_dtype=_BF)
        hp, rp, cp, kp, vp = _layer(hp, jnp.zeros((batch, RET_HEADS, RET_DK, RET_DV), _F32), None, RET_CHUNK,
                                    attend_p, wts, alpha, _BF, 512)
        ck = cache_k[l].reshape(n_phys, page, MOBA_W)
        cv = cache_v[l].reshape(n_phys, page, MOBA_W)
        attend_s = functools.partial(_moba_sample, slope_row=slope_row, cache_k=ck, cache_v=cv,
                                     page_table=page_table, t_new=dec_seq, out_dtype=_F32)
        hs, rs, cs, ks_, vs_ = _layer(hs, state_ret[l], state_conv[l], dec_seq, attend_s, wts, alpha, _F32, 128)
        for lst, val in zip(outs, (kp, vp, rp, cp, ks_, vs_, rs, cs)):
            lst.append(val)
    return (hp, hs) + tuple(jnp.stack(o) for o in outs)
```

```python
import functools

import jax
import jax.numpy as jnp
from jax import lax
from jax.experimental import pallas as pl
from jax.experimental.pallas import tpu as pltpu

D_MODEL = 1024
RET_HEADS = 4
RET_DK = 128
RET_DV = 256
RET_CHUNK = 128
MOBA_HEADS = 8
MOBA_DH = 64
MOBA_BLOCK = 256
MOBA_TOPK = 3
D_FF = 2816
CONV_W = 3
LN_EPS = 1e-5
GN_EPS = 1e-6

RET_QK_W = RET_HEADS * RET_DK
RET_V_W = RET_HEADS * RET_DV
MOBA_W = MOBA_HEADS * MOBA_DH
IN_SPLITS = (RET_QK_W, RET_QK_W, RET_V_W, RET_V_W, MOBA_W, MOBA_W, MOBA_W, D_MODEL, D_MODEL)
N_IN = sum(IN_SPLITS)

LANES = 128
VMEM_LIMIT_BYTES = 56 * 1024 * 1024
NEG_BIG = -1e30

_BF = jnp.bfloat16
_F32 = jnp.float32


def _nt(a, b):
    return lax.dot_general(a, b, (((1,), (1,)), ((), ())), preferred_element_type=_F32)


def _nn(a, b):
    return jnp.dot(a, b, preferred_element_type=_F32)


def _split_bf16(x):
    hi = x.astype(_BF)
    lo = (x - hi.astype(_F32)).astype(_BF)
    return hi, lo


def _nt3(a, b):
    ah, al = _split_bf16(a)
    bh, bl = _split_bf16(b)
    return _nt(ah, bh) + (_nt(al, bh) + _nt(ah, bl))


def _layer_norm(x, w, b):
    mu = jnp.mean(x, axis=-1, keepdims=True)
    d = x - mu
    var = jnp.mean(d * d, axis=-1, keepdims=True)
    return d * lax.rsqrt(var + LN_EPS) * w + b


def _const_spec(shape):
    return pl.BlockSpec(shape, lambda *_: (0,) * len(shape), pipeline_mode=pl.Buffered(1))


def _params(n_grid):
    return pltpu.CompilerParams(dimension_semantics=("arbitrary",) * n_grid,
                                vmem_limit_bytes=VMEM_LIMIT_BYTES)


def _in_proj_kernel(x_ref, w_ref, rq_ref, rk_ref, rv_ref, rg_ref, mq_ref, mk_ref, mv_ref, gab_ref):
    xb = x_ref[...].astype(_BF)
    offs = [0]
    for w in IN_SPLITS:
        offs.append(offs[-1] + w)

    def seg(i, j=None):
        return _nn(xb, w_ref[:, offs[i]:offs[(i if j is None else j) + 1]])

    rq_ref[...] = seg(0).astype(rq_ref.dtype)
    rk_ref[...] = (seg(1) * RET_DK ** -0.5).astype(rk_ref.dtype)
    rv_ref[...] = seg(2).astype(rv_ref.dtype)
    rg_ref[...] = seg(3).astype(rg_ref.dtype)
    mq_ref[...] = seg(4)
    mk_ref[...] = seg(5)
    mv_ref[...] = seg(6)
    gab_ref[...] = seg(7, 8).astype(gab_ref.dtype)


def _in_proj(x2d, w_in_bf, inter_dtype, tm):
    n = x2d.shape[0]
    assert n % tm == 0
    widths = (RET_QK_W, RET_QK_W, RET_V_W, RET_V_W, MOBA_W, MOBA_W, MOBA_W, 2 * D_MODEL)
    dtypes = (inter_dtype,) * 4 + (_F32,) * 3 + (inter_dtype,)
    return pl.pallas_call(
        _in_proj_kernel,
        grid=(n // tm,),
        in_specs=[pl.BlockSpec((tm, D_MODEL), lambda i: (i, 0)), _const_spec((D_MODEL, N_IN))],
        out_specs=[pl.BlockSpec((tm, w), lambda i: (i, 0)) for w in widths],
        out_shape=[jax.ShapeDtypeStruct((n, w), dt) for w, dt in zip(widths, dtypes)],
        compiler_params=_params(1),
        name="in_proj",
    )(x2d, w_in_bf)


def _retention_kernel(q_ref, k_ref, v_ref, g_ref, s0_ref, intra_ref, qdec_ref, kdec_ref, cdec_ref,
                      gnw_ref, ro_ref, sn_ref, *, seq, chunk, cpad, n_heads, n_seqs):
    n_chunks = seq // chunk
    sn_ref[...] = s0_ref[...]

    def pad(a):
        if chunk == cpad:
            return a
        return jnp.concatenate([a, jnp.zeros((cpad - chunk, a.shape[1]), a.dtype)], axis=0)

    for h in range(n_heads):
        intra = intra_ref[h]
        qdec = qdec_ref[h]
        kdec = kdec_ref[h]
        cdec = cdec_ref[h]
        kcols = slice(h * RET_DK, (h + 1) * RET_DK)
        vcols = slice(h * RET_DV, (h + 1) * RET_DV)
        gnw = gnw_ref[:, vcols]

        def body(it, carry):
            bb = it // n_chunks
            start = pl.multiple_of(it * chunk, chunk)
            rows = pl.ds(start, chunk)
            q = pad(q_ref[rows, kcols].astype(_F32))
            k = pad(k_ref[rows, kcols].astype(_F32))
            v = pad(v_ref[rows, vcols].astype(_F32)).astype(_BF)
            a = _nt(q.astype(_BF), k.astype(_BF)) * intra
            s = sn_ref[bb, h]
            o = _nn(a.astype(_BF), v) + _nn((q * qdec).astype(_BF), s.astype(_BF))
            sn_ref[bb, h] = cdec * s + _nn((k * kdec).T.astype(_BF), v)
            o = o[:chunk]
            mu = jnp.mean(o, axis=-1, keepdims=True)
            d = o - mu
            var = jnp.mean(d * d, axis=-1, keepdims=True)
            on = d * lax.rsqrt(var + GN_EPS) * gnw
            g = g_ref[rows, vcols].astype(_F32)
            ro_ref[rows, vcols] = (g * jax.nn.sigmoid(g) * on).astype(ro_ref.dtype)
            return carry

        lax.fori_loop(0, n_seqs * n_chunks, body, 0)


def _decay_tables(chunk, cpad):
    log_g = jnp.log1p(-jnp.exp2(-5.0 - jnp.arange(RET_HEADS, dtype=_F32)))
    i = jnp.arange(chunk, dtype=_F32)
    diff = i[:, None] - i[None, :]
    intra = jnp.where(diff >= 0, jnp.exp(jnp.maximum(diff, 0.0)[None] * log_g[:, None, None]), 0.0)
    q_dec = jnp.exp((i[None] + 1.0) * log_g[:, None])
    k_dec = jnp.exp((chunk - 1.0 - i[None]) * log_g[:, None])
    c_dec = jnp.exp(chunk * log_g)
    p = cpad - chunk
    intra = jnp.pad(intra, ((0, 0), (0, p), (0, p)))
    q_dec = jnp.broadcast_to(jnp.pad(q_dec, ((0, 0), (0, p)))[:, :, None], (RET_HEADS, cpad, RET_DK))
    k_dec = jnp.broadcast_to(jnp.pad(k_dec, ((0, 0), (0, p)))[:, :, None], (RET_HEADS, cpad, RET_DK))
    c_dec = jnp.broadcast_to(c_dec[:, None, None], (RET_HEADS, 1, RET_DV))
    return intra, q_dec, k_dec, c_dec


def _retention(rq, rk, rv, rg, s0, gn_w, batch, seq, chunk, out_dtype, n_heads, n_seqs):
    cpad = max(chunk, LANES)
    assert seq % chunk == 0 and batch % n_seqs == 0 and RET_HEADS % n_heads == 0
    intra, q_dec, k_dec, c_dec = _decay_tables(chunk, cpad)
    kern = functools.partial(_retention_kernel, seq=seq, chunk=chunk, cpad=cpad, n_heads=n_heads, n_seqs=n_seqs)
    head_tab = lambda shape: pl.BlockSpec((n_heads,) + shape, lambda b, h: (h, 0, 0))
    tok = lambda w: pl.BlockSpec((n_seqs * seq, n_heads * w), lambda b, h: (b, h))
    state = pl.BlockSpec((n_seqs, n_heads, RET_DK, RET_DV), lambda b, h: (b, h, 0, 0))
    return pl.pallas_call(
        kern,
        grid=(batch // n_seqs, RET_HEADS // n_heads),
        in_specs=[tok(RET_DK), tok(RET_DK), tok(RET_DV), tok(RET_DV), state,
                  head_tab((cpad, cpad)), head_tab((cpad, RET_DK)), head_tab((cpad, RET_DK)),
                  head_tab((1, RET_DV)),
                  pl.BlockSpec((1, n_heads * RET_DV), lambda b, h: (0, h))],
        out_specs=[tok(RET_DV), state],
        out_shape=[
            jax.ShapeDtypeStruct((batch * seq, RET_V_W), out_dtype),
            jax.ShapeDtypeStruct((batch, RET_HEADS, RET_DK, RET_DV), _F32),
        ],
        compiler_params=_params(2),
        name="retention",
    )(rq, rk, rv, rg, s0, intra, q_dec, k_dec, c_dec, gn_w)


def _moba_prompt_kernel(q_ref, k_ref, v_ref, slope_ref, o_ref, *, seq):
    blk = MOBA_BLOCK
    nb = seq // blk
    hw = MOBA_DH
    scale = MOBA_DH ** -0.5
    q = q_ref[...] * scale
    k = k_ref[...]
    kb = k.astype(_BF)
    vt = v_ref[...].T.astype(_BF)
    kmean = jnp.concatenate(
        [jnp.sum(k[n * blk:(n + 1) * blk], axis=0, keepdims=True) for n in range(nb)], axis=0) * (1.0 / blk)

    lane_head = lax.broadcasted_iota(jnp.int32, (blk, 2 * hw), 1) // hw
    key_i = lax.broadcasted_iota(jnp.int32, (blk, 2 * blk), 0)
    qry_i = lax.broadcasted_iota(jnp.int32, (blk, 2 * blk), 1) % blk
    causal = qry_i >= key_i
    col_head = lax.broadcasted_iota(jnp.int32, (1, 2 * blk), 1) // blk
    slope = jnp.where(col_head == 0, slope_ref[0:1, 0:1], slope_ref[0:1, hw:hw + 1])
    bias = -slope * (qry_i - key_i).astype(_F32)
    row_id = lax.broadcasted_iota(jnp.int32, (nb, 2 * blk), 0)

    for i in range(nb):
        rs = slice(i * blk, (i + 1) * blk)
        qi = q[rs]
        qexp = jnp.concatenate([jnp.where(lane_head == 0, qi, 0.0), jnp.where(lane_head == 1, qi, 0.0)], axis=0)
        st = _nt(kb[:(i + 1) * blk], qexp.astype(_BF))
        sel = None
        if i > MOBA_TOPK:
            gt = _nt3(kmean, qexp)
            valid = row_id < i
            sel = []
            for n in range(i):
                gn = gt[n:n + 1]
                ahead = ((gt > gn) | ((gt == gn) & (row_id < n))) & valid
                rank = jnp.sum(ahead.astype(_F32), axis=0, keepdims=True)
                sel.append(rank < MOBA_TOPK)
        ts, shifts, tops = [], [], []
        for n in range(i + 1):
            t = st[n * blk:(n + 1) * blk] + bias
            if n == i:
                t = jnp.where(causal, t, NEG_BIG)
            elif sel is not None:
                t = jnp.where(sel[n], t, NEG_BIG)
            shift = slope * float(-(i - n) * blk)
            ts.append(t)
            shifts.append(shift)
            tops.append(jnp.max(t, axis=0, keepdims=True) + shift)
        m = functools.reduce(jnp.maximum, tops)
        den = jnp.zeros((1, 2 * blk), _F32)
        ps = []
        for n in range(i + 1):
            e = jnp.exp(ts[n] + (shifts[n] - m))
            den = den + jnp.sum(e, axis=0, keepdims=True)
            ps.append(e.astype(_BF))
        pt = ps[0] if i == 0 else jnp.concatenate(ps, axis=0)
        ot = _nn(vt[:, :(i + 1) * blk], pt) * (1.0 / den)
        oi = jnp.concatenate([ot[0:hw, 0:blk], ot[hw:2 * hw, blk:2 * blk]], axis=0)
        o_ref[rs, :] = oi.T.astype(o_ref.dtype)


def _moba_prompt(mq, mk, mv, slope_row, batch, seq, out_dtype):
    width = 2 * MOBA_DH
    assert width == LANES and seq % MOBA_BLOCK == 0
    kern = functools.partial(_moba_prompt_kernel, seq=seq)
    spec = pl.BlockSpec((seq, width), lambda b, h: (b, h))
    return pl.pallas_call(
        kern,
        grid=(batch, MOBA_W // width),
        in_specs=[spec, spec, spec, pl.BlockSpec((1, width), lambda b, h: (0, h))],
        out_specs=spec,
        out_shape=jax.ShapeDtypeStruct((batch * seq, MOBA_W), out_dtype),
        compiler_params=_params(2),
        name="moba_prompt",
    )(mq, mk, mv, slope_row)


RING_SLOTS = 4
KEY_GROUPS = 4


def _moba_sample_kernel(pt_ref, q_ref, kn_ref, vn_ref, slope_ref, ck_hbm, cv_hbm, o_ref,
                        ring, s_scr, p_scr, bias_scr, sem, *, n_pages, page, t_new):
    b = pl.program_id(0)
    n_seq = pl.num_programs(0)
    blk = MOBA_BLOCK
    past = n_pages * page
    nb = past // blk
    rows = MOBA_HEADS * t_new
    scale = MOBA_DH ** -0.5
    ppc = n_pages // KEY_GROUPS
    cw = ppc * page
    bpc = cw // blk
    n_chunks = 2 * KEY_GROUPS
    lead = RING_SLOTS - 1

    def copies(seq_idx, c):
        src = ck_hbm if c < KEY_GROUPS else cv_hbm
        first = (c % KEY_GROUPS) * ppc
        slot = c % RING_SLOTS
        return [pltpu.make_async_copy(src.at[pt_ref[seq_idx, first + j]],
                                      ring.at[slot, :, pl.ds(j * page, page)], sem.at[slot])
                for j in range(ppc)]

    def start(seq_idx, c):
        for d in copies(seq_idx, c):
            d.start()

    q_rows = jnp.concatenate([q_ref[...]] * MOBA_HEADS, axis=0)
    r_head = lax.broadcasted_iota(jnp.int32, (rows, MOBA_W), 0) // t_new
    c_head = lax.broadcasted_iota(jnp.int32, (rows, MOBA_W), 1) // MOBA_DH
    own = r_head == c_head
    q_hi, q_lo = _split_bf16(jnp.where(own, q_rows, 0.0) * scale)
    q_stack = jnp.concatenate([q_hi, q_lo], axis=0)
    slope = jnp.sum(jnp.where(own, jnp.broadcast_to(slope_ref[...], (rows, MOBA_W)), 0.0),
                    axis=-1, keepdims=True) * (1.0 / MOBA_DH)
    tok = lax.broadcasted_iota(jnp.int32, (rows, blk), 0) % t_new
    off = lax.broadcasted_iota(jnp.int32, (rows, blk), 1)

    @pl.when(b == 0)
    def _():
        for c in range(lead):
            start(0, c)
        for n in range(nb):
            dist = (past - n * blk) + (tok - off)
            bias_scr[:, n * blk:(n + 1) * blk] = -slope * dist.astype(_F32)

    gate = jnp.zeros((rows, nb), _F32)
    bid = lax.broadcasted_iota(jnp.int32, (rows, nb), 1)
    acc = den = None
    for c in range(n_chunks):
        nxt = c + lead
        if nxt < n_chunks:
            start(b, nxt)
        else:
            @pl.when(b + 1 < n_seq)
            def _():
                start(b + 1, nxt - n_chunks)
        for d in copies(b, c):
            d.wait()
        slot = c % RING_SLOTS
        if c < KEY_GROUPS:
            kc = ring[slot]
            k_hi, k_lo = _split_bf16(kc)
            r1 = _nn(q_stack, k_hi)
            s_scr[:, c * cw:(c + 1) * cw] = r1[:rows]
            g = r1[:rows] + (r1[rows:] + _nn(q_hi, k_lo))
            for j in range(bpc):
                gs = jnp.sum(g[:, j * blk:(j + 1) * blk], axis=-1, keepdims=True)
                gate = jnp.where(bid == c * bpc + j, gs, gate)
        else:
            cc = c - KEY_GROUPS
            acc = acc + _nt(p_scr[:, cc * cw:(cc + 1) * cw], ring[slot].astype(_BF))
        if c == KEY_GROUPS - 1:
            sel = jnp.zeros((rows, nb), jnp.bool_)
            for _ in range(min(MOBA_TOPK, nb)):
                mx = jnp.max(gate, axis=-1, keepdims=True)
                first = jnp.min(jnp.where(gate == mx, bid, nb), axis=-1, keepdims=True)
                hit = bid == first
                sel = sel | hit
                gate = jnp.where(hit, -jnp.inf, gate)
            zpad = jnp.zeros((page - t_new, MOBA_W), _F32)
            kn = jnp.concatenate([kn_ref[...], zpad], axis=0).astype(_BF)
            vn = jnp.concatenate([vn_ref[...], zpad], axis=0).astype(_BF)
            tok_p = lax.broadcasted_iota(jnp.int32, (rows, page), 0) % t_new
            off_p = lax.broadcasted_iota(jnp.int32, (rows, page), 1)
            s_own = _nt(q_hi, kn) - slope * (tok_p - off_p).astype(_F32)
            s_own = jnp.where(off_p <= tok_p, s_own, NEG_BIG)
            mrun = jnp.full((rows, blk), NEG_BIG, _F32)
            for n in range(nb):
                cols = slice(n * blk, (n + 1) * blk)
                t = jnp.where(sel[:, n:n + 1], s_scr[:, cols] + bias_scr[:, cols], NEG_BIG)
                s_scr[:, cols] = t
                mrun = jnp.maximum(mrun, t)
            m = jnp.maximum(jnp.max(mrun, axis=-1, keepdims=True), jnp.max(s_own, axis=-1, keepdims=True))
            p_own = jnp.exp(s_own - m)
            lrun = jnp.zeros((rows, blk), _F32)
            for n in range(nb):
                cols = slice(n * blk, (n + 1) * blk)
                e = jnp.exp(s_scr[:, cols] - m)
                lrun = lrun + e
                p_scr[:, cols] = e.astype(_BF)
            den = jnp.sum(lrun, axis=-1, keepdims=True) + jnp.sum(p_own, axis=-1, keepdims=True)
            acc = _nn(p_own.astype(_BF), vn)

    out = acc * (1.0 / den)
    o_ref[...] = jnp.concatenate(
        [out[h * t_new:(h + 1) * t_new, h * MOBA_DH:(h + 1) * MOBA_DH] for h in range(MOBA_HEADS)],
        axis=1).astype(o_ref.dtype)


def _moba_sample(mq, mk, mv, slope_row, cache_kt, cache_vt, page_table, t_new, out_dtype):
    batch, n_pages = page_table.shape
    page = cache_kt.shape[2]
    assert page == LANES and t_new <= page and MOBA_BLOCK % page == 0
    assert n_pages % KEY_GROUPS == 0 and (n_pages // KEY_GROUPS * page) % MOBA_BLOCK == 0
    assert (2 * KEY_GROUPS) % RING_SLOTS == 0
    past = n_pages * page
    cw = n_pages // KEY_GROUPS * page
    rows = MOBA_HEADS * t_new
    kern = functools.partial(_moba_sample_kernel, n_pages=n_pages, page=page, t_new=t_new)
    tok = pl.BlockSpec((t_new, MOBA_W), lambda b, pt: (b, 0))
    grid_spec = pltpu.PrefetchScalarGridSpec(
        num_scalar_prefetch=1,
        grid=(batch,),
        in_specs=[tok, tok, tok, pl.BlockSpec((1, MOBA_W), lambda b, pt: (0, 0)),
                  pl.BlockSpec(memory_space=pl.ANY), pl.BlockSpec(memory_space=pl.ANY)],
        out_specs=tok,
        scratch_shapes=[
            pltpu.VMEM((RING_SLOTS, MOBA_W, cw), _F32),
            pltpu.VMEM((rows, past), _F32),
            pltpu.VMEM((rows, past), _BF),
            pltpu.VMEM((rows, past), _F32),
            pltpu.SemaphoreType.DMA((RING_SLOTS,)),
        ],
    )
    return pl.pallas_call(
        kern,
        grid_spec=grid_spec,
        out_shape=jax.ShapeDtypeStruct((batch * t_new, MOBA_W), out_dtype),
        compiler_params=_params(1),
        name="moba_sample",
    )(page_table, mq, mk, mv, slope_row, cache_kt, cache_vt)


def _merge_kernel(x_ref, ro_ref, mo_ref, gab_ref, wr_ref, wm_ref, wo_ref, lnw_ref, lnb_ref, h_ref, *, alpha):
    a = _nn(ro_ref[...].astype(_BF), wr_ref[...])
    m = _nn(mo_ref[...].astype(_BF), wm_ref[...])
    ga = gab_ref[:, :D_MODEL].astype(_F32)
    gb = gab_ref[:, D_MODEL:].astype(_F32)
    merged = jax.nn.sigmoid(ga) * a + jax.nn.sigmoid(gb) * m
    pre = alpha * x_ref[...] + _nn(merged.astype(_BF), wo_ref[...])
    h_ref[...] = _layer_norm(pre, lnw_ref[...], lnb_ref[...])


def _merge(x2d, ro, mo, gab, w_ret_out, w_moba_out, w_o, ln_w, ln_b, alpha, tm):
    n = x2d.shape[0]
    assert n % tm == 0
    row = lambda w: pl.BlockSpec((tm, w), lambda i: (i, 0))
    return pl.pallas_call(
        functools.partial(_merge_kernel, alpha=alpha),
        grid=(n // tm,),
        in_specs=[row(D_MODEL), row(RET_V_W), row(MOBA_W), row(2 * D_MODEL),
                  _const_spec((RET_V_W, D_MODEL)), _const_spec((MOBA_W, D_MODEL)),
                  _const_spec((D_MODEL, D_MODEL)), _const_spec((1, D_MODEL)), _const_spec((1, D_MODEL))],
        out_specs=row(D_MODEL),
        out_shape=jax.ShapeDtypeStruct((n, D_MODEL), _F32),
        compiler_params=_params(1),
        name="merge",
    )(x2d, ro, mo, gab, w_ret_out, w_moba_out, w_o, ln_w, ln_b)


FFN_COL_CHUNK = D_FF // 2
CONV_PAD = 8


def _ffn_kernel(*refs, alpha, tm, seq, within_seq):
    if within_seq:
        (h_ref, wup_ref, cw_ref, cb_ref, wdn_ref, lnw_ref, lnb_ref, y_ref, cs_ref, ubuf) = refs
    else:
        (h_ref, p1_ref, p2_ref, wup_ref, cw_ref, cb_ref, wdn_ref, lnw_ref, lnb_ref, y_ref, ua_ref, ubuf) = refs
    h = h_ref[...]
    hb = h.astype(_BF)
    hist = CONV_W - 1

    if within_seq:
        tiles_per_seq = seq // tm

        @pl.when(pl.program_id(0) % tiles_per_seq == 0)
        def _():
            ubuf[0:CONV_PAD, :] = jnp.zeros((CONV_PAD, D_FF), _F32)
    else:
        t_in_seq = lax.broadcasted_iota(jnp.int32, (tm, 1), 0) % seq

        @pl.when(pl.program_id(0) == 0)
        def _():
            ubuf[0:CONV_PAD, :] = jnp.zeros((CONV_PAD, D_FF), _F32)

    f = jnp.zeros((tm, D_MODEL), _F32)
    for c0 in range(0, D_FF, FFN_COL_CHUNK):
        cols = slice(c0, c0 + FFN_COL_CHUNK)
        ua = _nn(hb, wup_ref[:, c0:c0 + FFN_COL_CHUNK])
        ug = _nn(hb, wup_ref[:, D_FF + c0:D_FF + c0 + FFN_COL_CHUNK])
        ubuf[CONV_PAD:CONV_PAD + tm, cols] = ua
        sh1 = ubuf[CONV_PAD - 1:CONV_PAD - 1 + tm, cols]
        sh2 = ubuf[CONV_PAD - 2:CONV_PAD - 2 + tm, cols]
        if not within_seq:
            sh1 = jnp.where(t_in_seq >= 1, sh1, p1_ref[:, cols])
            sh2 = jnp.where(t_in_seq >= 2, sh2, p2_ref[:, cols])
            ua_ref[:, cols] = ua
        uc = cb_ref[:, cols] + (sh2 * cw_ref[0:1, cols] + sh1 * cw_ref[1:2, cols] + ua * cw_ref[2:3, cols])
        act = (uc * jax.nn.sigmoid(uc) * ug).astype(_BF)
        f = f + _nn(act, wdn_ref[c0:c0 + FFN_COL_CHUNK, :])

    if within_seq:
        last = ubuf[CONV_PAD + tm - hist:CONV_PAD + tm, :]
        cs_ref[0] = last
        ubuf[CONV_PAD - hist:CONV_PAD, :] = last
    y_ref[...] = _layer_norm(alpha * h + f, lnw_ref[...], lnb_ref[...])


def _ffn_prompt(h2d, w_up, conv_w, conv_b, w_down, ln_w, ln_b, alpha, batch, seq, tm):
    n = h2d.shape[0]
    assert seq % tm == 0 and tm >= CONV_W - 1
    tiles_per_seq = seq // tm
    row = pl.BlockSpec((tm, D_MODEL), lambda i: (i, 0))
    return pl.pallas_call(
        functools.partial(_ffn_kernel, alpha=alpha, tm=tm, seq=seq, within_seq=True),
        grid=(n // tm,),
        in_specs=[row, _const_spec((D_MODEL, 2 * D_FF)), _const_spec((CONV_W, D_FF)), _const_spec((1, D_FF)),
                  _const_spec((D_FF, D_MODEL)), _const_spec((1, D_MODEL)), _const_spec((1, D_MODEL))],
        out_specs=[row, pl.BlockSpec((1, CONV_W - 1, D_FF), lambda i: (i // tiles_per_seq, 0, 0))],
        out_shape=[jax.ShapeDtypeStruct((n, D_MODEL), _F32),
                   jax.ShapeDtypeStruct((batch, CONV_W - 1, D_FF), _F32)],
        scratch_shapes=[pltpu.VMEM((CONV_PAD + tm, D_FF), _F32)],
        compiler_params=_params(1),
        name="ffn_prompt",
    )(h2d, w_up, conv_w, conv_b, w_down, ln_w, ln_b)


def _ffn_sample(h2d, prev1, prev2, w_up, conv_w, conv_b, w_down, ln_w, ln_b, alpha, seq, tm):
    n = h2d.shape[0]
    assert n % tm == 0 and tm % seq == 0
    row = pl.BlockSpec((tm, D_MODEL), lambda i: (i, 0))
    wide = pl.BlockSpec((tm, D_FF), lambda i: (i, 0))
    return pl.pallas_call(
        functools.partial(_ffn_kernel, alpha=alpha, tm=tm, seq=seq, within_seq=False),
        grid=(n // tm,),
        in_specs=[row, wide, wide, _const_spec((D_MODEL, 2 * D_FF)), _const_spec((CONV_W, D_FF)),
                  _const_spec((1, D_FF)), _const_spec((D_FF, D_MODEL)), _const_spec((1, D_MODEL)),
                  _const_spec((1, D_MODEL))],
        out_specs=[row, wide],
        out_shape=[jax.ShapeDtypeStruct((n, D_MODEL), _F32), jax.ShapeDtypeStruct((n, D_FF), _F32)],
        scratch_shapes=[pltpu.VMEM((CONV_PAD + tm, D_FF), _F32)],
        compiler_params=_params(1),
        name="ffn_sample",
    )(h2d, prev1, prev2, w_up, conv_w, conv_b, w_down, ln_w, ln_b)


def _pick_tile(n, cap):
    t = cap
    while n % t:
        t //= 2
    return t


def _layer(x, ret_state, conv_state, ret_chunk, ret_tiling, attend, wts, alpha, inter_dtype, row_tile):
    (w_in, gn_w, w_ret_out, w_moba_out, w_o, ln1_w, ln1_b, w_up, conv_w, conv_b, w_down, ln2_w, ln2_b) = wts
    batch, seq, _ = x.shape
    n = batch * seq
    x2d = x.reshape(n, D_MODEL)
    tm = _pick_tile(n, row_tile)
    rq, rk, rv, rg, mq, mk, mv, gab = _in_proj(x2d, w_in, inter_dtype, tm)
    ro, ret_new = _retention(rq, rk, rv, rg, ret_state, gn_w, batch, seq, ret_chunk, inter_dtype, *ret_tiling)
    mo = attend(mq, mk, mv)
    h = _merge(x2d, ro, mo, gab, w_ret_out, w_moba_out, w_o, ln1_w, ln1_b, alpha, tm)
    if conv_state is None:
        y, conv_new = _ffn_prompt(h, w_up, conv_w, conv_b, w_down, ln2_w, ln2_b, alpha, batch, seq,
                                  _pick_tile(seq, row_tile))
    else:
        hist = CONV_W - 1
        prev1 = jnp.zeros((batch, seq, D_FF), _F32).at[:, 0].set(conv_state[:, hist - 1])
        prev2 = jnp.zeros((batch, seq, D_FF), _F32).at[:, 0].set(conv_state[:, hist - 2])
        prev2 = prev2.at[:, 1].set(conv_state[:, hist - 1])
        y, ua = _ffn_sample(h, prev1.reshape(n, D_FF), prev2.reshape(n, D_FF), w_up, conv_w, conv_b, w_down,
                            ln2_w, ln2_b, alpha, seq, tm)
        conv_new = jnp.concatenate([conv_state, ua.reshape(batch, seq, D_FF)], axis=1)[:, -hist:]
    shp = (batch, seq, MOBA_HEADS, MOBA_DH)
    return y.reshape(batch, seq, D_MODEL), ret_new, conv_new, mk.reshape(shp), mv.reshape(shp)


def kernel(x_prompt, x_sample, cache_k, cache_v, page_table, state_ret, state_conv, w_in, ret_gn_w, w_ret_out,
           w_moba_out, w_o, ln1_w, ln1_b, w_up, conv_w, conv_b, w_down, ln2_w, ln2_b):
    depth = w_in.shape[0]
    alpha = (2 * depth) ** 0.25
    batch, seq, _ = x_prompt.shape
    dec_batch, dec_seq, _ = x_sample.shape
    n_phys, page = cache_k.shape[1], cache_k.shape[2]
    slope_row = jnp.repeat(jnp.exp2(-8.0 * (jnp.arange(MOBA_HEADS, dtype=_F32) + 1.0) / MOBA_HEADS),
                           MOBA_DH)[None, :]
    hp, hs = x_prompt, x_sample
    outs = [[] for _ in range(8)]
    for l in range(depth):
        wts = (w_in[l].astype(_BF), ret_gn_w[l][None, :], w_ret_out[l].astype(_BF), w_moba_out[l].astype(_BF),
               w_o[l].astype(_BF), ln1_w[l][None, :], ln1_b[l][None, :], w_up[l].astype(_BF), conv_w[l],
               conv_b[l][None, :], w_down[l].astype(_BF), ln2_w[l][None, :], ln2_b[l][None, :])
        attend_p = functools.partial(_moba_prompt, slope_row=slope_row, batch=batch, seq=seq, out_dtype=_BF)
        hp, rp, cp, kp, vp = _layer(hp, jnp.zeros((batch, RET_HEADS, RET_DK, RET_DV), _F32), None, RET_CHUNK,
                                    (1, 1), attend_p, wts, alpha, _BF, 512)
        ck = jnp.transpose(cache_k[l], (0, 2, 3, 1)).reshape(n_phys, MOBA_W, page)
        cv = jnp.transpose(cache_v[l], (0, 2, 3, 1)).reshape(n_phys, MOBA_W, page)
        attend_s = functools.partial(_moba_sample, slope_row=slope_row, cache_kt=ck, cache_vt=cv,
                                     page_table=page_table, t_new=dec_seq, out_dtype=_F32)
        hs, rs, cs, ks_, vs_ = _layer(hs, state_ret[l], state_conv[l], dec_seq, (RET_HEADS, _pick_tile(dec_batch, 8)),
                                      attend_s, wts, alpha, _F32, 128)
        for lst, val in zip(outs, (kp, vp, rp, cp, ks_, vs_, rs, cs)):
            lst.append(val)
    return (hp, hs) + tuple(jnp.stack(o) for o in outs)
```

```python
import functools

import jax
import jax.numpy as jnp
from jax import lax
from jax.experimental import pallas as pl
from jax.experimental.pallas import tpu as pltpu

D_MODEL = 1024
RET_HEADS = 4
RET_DK = 128
RET_DV = 256
RET_CHUNK = 128
MOBA_HEADS = 8
MOBA_DH = 64
MOBA_BLOCK = 256
MOBA_TOPK = 3
D_FF = 2816
CONV_W = 3
LN_EPS = 1e-5
GN_EPS = 1e-6

RET_QK_W = RET_HEADS * RET_DK
RET_V_W = RET_HEADS * RET_DV
MOBA_W = MOBA_HEADS * MOBA_DH
IN_SPLITS = (RET_QK_W, RET_QK_W, RET_V_W, RET_V_W, MOBA_W, MOBA_W, MOBA_W, D_MODEL, D_MODEL)
N_IN = sum(IN_SPLITS)

LANES = 128
VMEM_LIMIT_BYTES = 56 * 1024 * 1024
NEG_BIG = -1e30
LOG2E = 1.4426950408889634
ONES_ROWS = 16

_BF = jnp.bfloat16
_F32 = jnp.float32


def _nt(a, b):
    return lax.dot_general(a, b, (((1,), (1,)), ((), ())), preferred_element_type=_F32)


def _nn(a, b):
    return jnp.dot(a, b, preferred_element_type=_F32)


def _split_bf16(x):
    hi = x.astype(_BF)
    lo = (x - hi.astype(_F32)).astype(_BF)
    return hi, lo


def _nt3(a, b):
    ah, al = _split_bf16(a)
    bh, bl = _split_bf16(b)
    return _nt(ah, bh) + (_nt(al, bh) + _nt(ah, bl))


def _layer_norm(x, w, b):
    mu = jnp.mean(x, axis=-1, keepdims=True)
    d = x - mu
    var = jnp.mean(d * d, axis=-1, keepdims=True)
    return d * lax.rsqrt(var + LN_EPS) * w + b


def _const_spec(shape):
    return pl.BlockSpec(shape, lambda *_: (0,) * len(shape), pipeline_mode=pl.Buffered(1))


def _params(n_grid):
    return pltpu.CompilerParams(dimension_semantics=("arbitrary",) * n_grid,
                                vmem_limit_bytes=VMEM_LIMIT_BYTES)


def _in_proj_kernel(x_ref, w_ref, rq_ref, rk_ref, rv_ref, rg_ref, mq_ref, mk_ref, mv_ref, gab_ref):
    xb = x_ref[...].astype(_BF)
    offs = [0]
    for w in IN_SPLITS:
        offs.append(offs[-1] + w)

    def seg(i, j=None):
        return _nn(xb, w_ref[:, offs[i]:offs[(i if j is None else j) + 1]])

    rq_ref[...] = seg(0).astype(rq_ref.dtype)
    rk_ref[...] = (seg(1) * RET_DK ** -0.5).astype(rk_ref.dtype)
    rv_ref[...] = seg(2).astype(rv_ref.dtype)
    rg_ref[...] = seg(3).astype(rg_ref.dtype)
    mq_ref[...] = seg(4)
    if len(mk_ref.shape) == 3:
        mk_ref[0] = seg(5).T
        mv_ref[0] = seg(6).T
    else:
        mk_ref[...] = seg(5)
        mv_ref[...] = seg(6)
    gab_ref[...] = seg(7, 8).astype(gab_ref.dtype)


def _in_proj(x2d, w_in_bf, inter_dtype, tm, kv_time_minor_seq=None):
    n = x2d.shape[0]
    assert n % tm == 0
    widths = (RET_QK_W, RET_QK_W, RET_V_W, RET_V_W, MOBA_W, MOBA_W, MOBA_W, 2 * D_MODEL)
    dtypes = (inter_dtype,) * 4 + (_F32,) * 3 + (inter_dtype,)
    out_specs = [pl.BlockSpec((tm, w), lambda i: (i, 0)) for w in widths]
    out_shape = [jax.ShapeDtypeStruct((n, w), dt) for w, dt in zip(widths, dtypes)]
    if kv_time_minor_seq is not None:
        seq = kv_time_minor_seq
        assert seq % tm == 0 and tm % LANES == 0
        tps = seq // tm
        for j in (5, 6):
            out_specs[j] = pl.BlockSpec((1, MOBA_W, tm), lambda i: (i // tps, 0, i % tps))
            out_shape[j] = jax.ShapeDtypeStruct((n // seq, MOBA_W, seq), _F32)
    return pl.pallas_call(
        _in_proj_kernel,
        grid=(n // tm,),
        in_specs=[pl.BlockSpec((tm, D_MODEL), lambda i: (i, 0)), _const_spec((D_MODEL, N_IN))],
        out_specs=out_specs,
        out_shape=out_shape,
        compiler_params=_params(1),
        name="in_proj",
    )(x2d, w_in_bf)


def _retention_kernel(q_ref, k_ref, v_ref, g_ref, s0_ref, intra_ref, qdec_ref, kdec_ref, cdec_ref,
                      gnw_ref, ro_ref, sn_ref, *, seq, chunk, cpad, n_heads, n_seqs):
    n_chunks = seq // chunk

    def pad(a):
        if chunk == cpad:
            return a
        return jnp.concatenate([a, jnp.zeros((cpad - chunk, a.shape[1]), a.dtype)], axis=0)

    for h in range(n_heads):
        intra = intra_ref[h]
        qdec = qdec_ref[h]
        kdec = kdec_ref[h]
        cdec = cdec_ref[h]
        kcols = slice(h * RET_DK, (h + 1) * RET_DK)
        vcols = slice(h * RET_DV, (h + 1) * RET_DV)
        gnw = gnw_ref[:, vcols]
        for bb in range(n_seqs):
            s = s0_ref[bb, h]
            for c in range(n_chunks):
                rows = slice(bb * seq + c * chunk, bb * seq + (c + 1) * chunk)
                q = pad(q_ref[rows, kcols].astype(_F32))
                k = pad(k_ref[rows, kcols].astype(_F32))
                v = pad(v_ref[rows, vcols].astype(_F32)).astype(_BF)
                a = _nt(q.astype(_BF), k.astype(_BF)) * intra
                o = _nn(a.astype(_BF), v) + _nn((q * qdec).astype(_BF), s.astype(_BF))
                s = cdec * s + _nn((k * kdec).T.astype(_BF), v)
                o = o[:chunk]
                mu = jnp.mean(o, axis=-1, keepdims=True)
                d = o - mu
                var = jnp.mean(d * d, axis=-1, keepdims=True)
                on = d * lax.rsqrt(var + GN_EPS) * gnw
                g = g_ref[rows, vcols].astype(_F32)
                ro_ref[rows, vcols] = (g * jax.nn.sigmoid(g) * on).astype(ro_ref.dtype)
            sn_ref[bb, h] = s


def _decay_tables(chunk, cpad):
    log_g = jnp.log1p(-jnp.exp2(-5.0 - jnp.arange(RET_HEADS, dtype=_F32)))
    i = jnp.arange(chunk, dtype=_F32)
    diff = i[:, None] - i[None, :]
    intra = jnp.where(diff >= 0, jnp.exp(jnp.maximum(diff, 0.0)[None] * log_g[:, None, None]), 0.0)
    q_dec = jnp.exp((i[None] + 1.0) * log_g[:, None])
    k_dec = jnp.exp((chunk - 1.0 - i[None]) * log_g[:, None])
    c_dec = jnp.exp(chunk * log_g)
    p = cpad - chunk
    intra = jnp.pad(intra, ((0, 0), (0, p), (0, p)))
    q_dec = jnp.broadcast_to(jnp.pad(q_dec, ((0, 0), (0, p)))[:, :, None], (RET_HEADS, cpad, RET_DK))
    k_dec = jnp.broadcast_to(jnp.pad(k_dec, ((0, 0), (0, p)))[:, :, None], (RET_HEADS, cpad, RET_DK))
    c_dec = jnp.broadcast_to(c_dec[:, None, None], (RET_HEADS, 1, RET_DV))
    return intra, q_dec, k_dec, c_dec


def _retention(rq, rk, rv, rg, s0, gn_w, batch, seq, chunk, out_dtype, n_heads, n_seqs):
    cpad = max(chunk, LANES)
    assert seq % chunk == 0 and batch % n_seqs == 0 and RET_HEADS % n_heads == 0
    intra, q_dec, k_dec, c_dec = _decay_tables(chunk, cpad)
    kern = functools.partial(_retention_kernel, seq=seq, chunk=chunk, cpad=cpad, n_heads=n_heads, n_seqs=n_seqs)
    head_tab = lambda shape: pl.BlockSpec((n_heads,) + shape, lambda b, h: (h, 0, 0))
    tok = lambda w: pl.BlockSpec((n_seqs * seq, n_heads * w), lambda b, h: (b, h))
    state = pl.BlockSpec((n_seqs, n_heads, RET_DK, RET_DV), lambda b, h: (b, h, 0, 0))
    return pl.pallas_call(
        kern,
        grid=(batch // n_seqs, RET_HEADS // n_heads),
        in_specs=[tok(RET_DK), tok(RET_DK), tok(RET_DV), tok(RET_DV), state,
                  head_tab((cpad, cpad)), head_tab((cpad, RET_DK)), head_tab((cpad, RET_DK)),
                  head_tab((1, RET_DV)),
                  pl.BlockSpec((1, n_heads * RET_DV), lambda b, h: (0, h))],
        out_specs=[tok(RET_DV), state],
        out_shape=[
            jax.ShapeDtypeStruct((batch * seq, RET_V_W), out_dtype),
            jax.ShapeDtypeStruct((batch, RET_HEADS, RET_DK, RET_DV), _F32),
        ],
        compiler_params=_params(2),
        name="retention",
    )(rq, rk, rv, rg, s0, intra, q_dec, k_dec, c_dec, gn_w)


def _moba_prompt_kernel(q_ref, kt_ref, vt_ref, slope_ref, o_ref, *, seq):
    blk = MOBA_BLOCK
    nb = seq // blk
    hw = MOBA_DH
    scale = MOBA_DH ** -0.5
    q = q_ref[...] * (scale * LOG2E)
    k = kt_ref[0].T
    kb = k.astype(_BF)
    vt = jnp.concatenate([vt_ref[0].astype(_BF), jnp.ones((ONES_ROWS, seq), _BF)], axis=0)
    kmean = jnp.concatenate(
        [jnp.sum(k[n * blk:(n + 1) * blk], axis=0, keepdims=True) for n in range(nb)], axis=0) * (1.0 / blk)

    lane_head = lax.broadcasted_iota(jnp.int32, (blk, 2 * hw), 1) // hw
    key_i = lax.broadcasted_iota(jnp.int32, (blk, 2 * blk), 0)
    qry_i = lax.broadcasted_iota(jnp.int32, (blk, 2 * blk), 1) % blk
    causal = qry_i >= key_i
    col_head = lax.broadcasted_iota(jnp.int32, (1, 2 * blk), 1) // blk
    slope = jnp.where(col_head == 0, slope_ref[0:1, 0:1], slope_ref[0:1, hw:hw + 1]) * LOG2E
    bias = -slope * (qry_i - key_i).astype(_F32)
    row_id = lax.broadcasted_iota(jnp.int32, (nb, 2 * blk), 0)

    for i in range(nb):
        rs = slice(i * blk, (i + 1) * blk)
        qi = q[rs]
        qexp = jnp.concatenate([jnp.where(lane_head == 0, qi, 0.0), jnp.where(lane_head == 1, qi, 0.0)], axis=0)
        st = _nt(kb[:(i + 1) * blk], qexp.astype(_BF))
        sel = None
        if i > MOBA_TOPK:
            gt = _nt3(kmean, qexp)
            valid = row_id < i
            sel = []
            for n in range(i):
                gn = gt[n:n + 1]
                ahead = ((gt > gn) | ((gt == gn) & (row_id < n))) & valid
                rank = jnp.sum(ahead.astype(_F32), axis=0, keepdims=True)
                sel.append(rank < MOBA_TOPK)
        ts, shifts, tops = [], [], []
        for n in range(i + 1):
            t = st[n * blk:(n + 1) * blk] + bias
            if n == i:
                t = jnp.where(causal, t, NEG_BIG)
            elif sel is not None:
                t = jnp.where(sel[n], t, NEG_BIG)
            shift = slope * float(-(i - n) * blk)
            ts.append(t)
            shifts.append(shift)
            tops.append(jnp.max(t, axis=0, keepdims=True) + shift)
        m = functools.reduce(jnp.maximum, tops)
        ps = [jnp.exp2(ts[n] + (shifts[n] - m)).astype(_BF) for n in range(i + 1)]
        pt = ps[0] if i == 0 else jnp.concatenate(ps, axis=0)
        ot = _nn(vt[:, :(i + 1) * blk], pt)
        ot = ot[:2 * hw] * (1.0 / ot[2 * hw:2 * hw + 1])
        oi = jnp.concatenate([ot[0:hw, 0:blk], ot[hw:2 * hw, blk:2 * blk]], axis=0)
        o_ref[rs, :] = oi.T.astype(o_ref.dtype)


def _moba_prompt(mq, mkt, mvt, slope_row, batch, seq, out_dtype):
    width = 2 * MOBA_DH
    assert width == LANES and seq % MOBA_BLOCK == 0
    kern = functools.partial(_moba_prompt_kernel, seq=seq)
    spec = pl.BlockSpec((seq, width), lambda b, h: (b, h))
    spec_t = pl.BlockSpec((1, width, seq), lambda b, h: (b, h, 0))
    return pl.pallas_call(
        kern,
        grid=(batch, MOBA_W // width),
        in_specs=[spec, spec_t, spec_t, pl.BlockSpec((1, width), lambda b, h: (0, h))],
        out_specs=spec,
        out_shape=jax.ShapeDtypeStruct((batch * seq, MOBA_W), out_dtype),
        compiler_params=_params(2),
        name="moba_prompt",
    )(mq, mkt, mvt, slope_row)


RING_SLOTS = 4
KEY_GROUPS = 4


def _moba_sample_kernel(pt_ref, q_ref, kn_ref, vn_ref, slope_ref, ck_hbm, cv_hbm, o_ref,
                        ring, s_scr, p_scr, bias_scr, sem, *, n_pages, page, t_new):
    b = pl.program_id(0)
    n_seq = pl.num_programs(0)
    blk = MOBA_BLOCK
    past = n_pages * page
    nb = past // blk
    rows = MOBA_HEADS * t_new
    scale = MOBA_DH ** -0.5
    ppc = n_pages // KEY_GROUPS
    cw = ppc * page
    bpc = cw // blk
    n_chunks = 2 * KEY_GROUPS
    lead = RING_SLOTS - 1

    def copies(seq_idx, c):
        src = ck_hbm if c < KEY_GROUPS else cv_hbm
        first = (c % KEY_GROUPS) * ppc
        slot = c % RING_SLOTS
        return [pltpu.make_async_copy(src.at[pt_ref[seq_idx, first + j]],
                                      ring.at[slot, :, pl.ds(j * page, page)], sem.at[slot])
                for j in range(ppc)]

    def start(seq_idx, c):
        for d in copies(seq_idx, c):
            d.start()

    q_rows = jnp.concatenate([q_ref[...]] * MOBA_HEADS, axis=0)
    r_head = lax.broadcasted_iota(jnp.int32, (rows, MOBA_W), 0) // t_new
    c_head = lax.broadcasted_iota(jnp.int32, (rows, MOBA_W), 1) // MOBA_DH
    own = r_head == c_head
    q_hi, q_lo = _split_bf16(jnp.where(own, q_rows, 0.0) * scale)
    q_stack = jnp.concatenate([q_hi, q_lo], axis=0)
    slope = jnp.sum(jnp.where(own, jnp.broadcast_to(slope_ref[...], (rows, MOBA_W)), 0.0),
                    axis=-1, keepdims=True) * (1.0 / MOBA_DH)
    tok = lax.broadcasted_iota(jnp.int32, (rows, blk), 0) % t_new
    off = lax.broadcasted_iota(jnp.int32, (rows, blk), 1)

    @pl.when(b == 0)
    def _():
        for c in range(lead):
            start(0, c)
        for n in range(nb):
            dist = (past - n * blk) + (tok - off)
            bias_scr[:, n * blk:(n + 1) * blk] = -slope * dist.astype(_F32)

    gate = jnp.zeros((rows, nb), _F32)
    bid = lax.broadcasted_iota(jnp.int32, (rows, nb), 1)
    acc = den = None
    for c in range(n_chunks):
        nxt = c + lead
        if nxt < n_chunks:
            start(b, nxt)
        else:
            @pl.when(b + 1 < n_seq)
            def _():
                start(b + 1, nxt - n_chunks)
        for d in copies(b, c):
            d.wait()
        slot = c % RING_SLOTS
        if c < KEY_GROUPS:
            kc = ring[slot]
            k_hi, k_lo = _split_bf16(kc)
            r1 = _nn(q_stack, k_hi)
            s_scr[:, c * cw:(c + 1) * cw] = r1[:rows]
            g = r1[:rows] + (r1[rows:] + _nn(q_hi, k_lo))
            for j in range(bpc):
                gs = jnp.sum(g[:, j * blk:(j + 1) * blk], axis=-1, keepdims=True)
                gate = jnp.where(bid == c * bpc + j, gs, gate)
        else:
            cc = c - KEY_GROUPS
            acc = acc + _nt(p_scr[:, cc * cw:(cc + 1) * cw], ring[slot].astype(_BF))
        if c == KEY_GROUPS - 1:
            sel = jnp.zeros((rows, nb), jnp.bool_)
            for _ in range(min(MOBA_TOPK, nb)):
                mx = jnp.max(gate, axis=-1, keepdims=True)
                first = jnp.min(jnp.where(gate == mx, bid, nb), axis=-1, keepdims=True)
                hit = bid == first
                sel = sel | hit
                gate = jnp.where(hit, -jnp.inf, gate)
            zpad = jnp.zeros((page - t_new, MOBA_W), _F32)
            kn = jnp.concatenate([kn_ref[...], zpad], axis=0).astype(_BF)
            vn = jnp.concatenate([vn_ref[...], zpad], axis=0).astype(_BF)
            tok_p = lax.broadcasted_iota(jnp.int32, (rows, page), 0) % t_new
            off_p = lax.broadcasted_iota(jnp.int32, (rows, page), 1)
            s_own = _nt(q_hi, kn) - slope * (tok_p - off_p).astype(_F32)
            s_own = jnp.where(off_p <= tok_p, s_own, NEG_BIG)
            mrun = jnp.full((rows, blk), NEG_BIG, _F32)
            for n in range(nb):
                cols = slice(n * blk, (n + 1) * blk)
                t = jnp.where(sel[:, n:n + 1], s_scr[:, cols] + bias_scr[:, cols], NEG_BIG)
                s_scr[:, cols] = t
                mrun = jnp.maximum(mrun, t)
            m = jnp.maximum(jnp.max(mrun, axis=-1, keepdims=True), jnp.max(s_own, axis=-1, keepdims=True))
            p_own = jnp.exp(s_own - m)
            lrun = jnp.zeros((rows, blk), _F32)
            for n in range(nb):
                cols = slice(n * blk, (n + 1) * blk)
                e = jnp.exp(s_scr[:, cols] - m)
                lrun = lrun + e
                p_scr[:, cols] = e.astype(_BF)
            den = jnp.sum(lrun, axis=-1, keepdims=True) + jnp.sum(p_own, axis=-1, keepdims=True)
            acc = _nn(p_own.astype(_BF), vn)

    out = acc * (1.0 / den)
    o_ref[...] = jnp.concatenate(
        [out[h * t_new:(h + 1) * t_new, h * MOBA_DH:(h + 1) * MOBA_DH] for h in range(MOBA_HEADS)],
        axis=1).astype(o_ref.dtype)


def _moba_sample(mq, mk, mv, slope_row, cache_kt, cache_vt, page_table, t_new, out_dtype):
    batch, n_pages = page_table.shape
    page = cache_kt.shape[2]
    assert page == LANES and t_new <= page and MOBA_BLOCK % page == 0
    assert n_pages % KEY_GROUPS == 0 and (n_pages // KEY_GROUPS * page) % MOBA_BLOCK == 0
    assert (2 * KEY_GROUPS) % RING_SLOTS == 0
    past = n_pages * page
    cw = n_pages // KEY_GROUPS * page
    rows = MOBA_HEADS * t_new
    kern = functools.partial(_moba_sample_kernel, n_pages=n_pages, page=page, t_new=t_new)
    tok = pl.BlockSpec((t_new, MOBA_W), lambda b, pt: (b, 0))
    grid_spec = pltpu.PrefetchScalarGridSpec(
        num_scalar_prefetch=1,
        grid=(batch,),
        in_specs=[tok, tok, tok, pl.BlockSpec((1, MOBA_W), lambda b, pt: (0, 0)),
                  pl.BlockSpec(memory_space=pl.ANY), pl.BlockSpec(memory_space=pl.ANY)],
        out_specs=tok,
        scratch_shapes=[
            pltpu.VMEM((RING_SLOTS, MOBA_W, cw), _F32),
            pltpu.VMEM((rows, past), _F32),
            pltpu.VMEM((rows, past), _BF),
            pltpu.VMEM((rows, past), _F32),
            pltpu.SemaphoreType.DMA((RING_SLOTS,)),
        ],
    )
    return pl.pallas_call(
        kern,
        grid_spec=grid_spec,
        out_shape=jax.ShapeDtypeStruct((batch * t_new, MOBA_W), out_dtype),
        compiler_params=_params(1),
        name="moba_sample",
    )(page_table, mq, mk, mv, slope_row, cache_kt, cache_vt)


def _merge_kernel(x_ref, ro_ref, mo_ref, gab_ref, wr_ref, wm_ref, wo_ref, lnw_ref, lnb_ref, h_ref, *, alpha):
    a = _nn(ro_ref[...].astype(_BF), wr_ref[...])
    m = _nn(mo_ref[...].astype(_BF), wm_ref[...])
    ga = gab_ref[:, :D_MODEL].astype(_F32)
    gb = gab_ref[:, D_MODEL:].astype(_F32)
    merged = jax.nn.sigmoid(ga) * a + jax.nn.sigmoid(gb) * m
    pre = alpha * x_ref[...] + _nn(merged.astype(_BF), wo_ref[...])
    h_ref[...] = _layer_norm(pre, lnw_ref[...], lnb_ref[...])


def _merge(x2d, ro, mo, gab, w_ret_out, w_moba_out, w_o, ln_w, ln_b, alpha, tm):
    n = x2d.shape[0]
    assert n % tm == 0
    row = lambda w: pl.BlockSpec((tm, w), lambda i: (i, 0))
    return pl.pallas_call(
        functools.partial(_merge_kernel, alpha=alpha),
        grid=(n // tm,),
        in_specs=[row(D_MODEL), row(RET_V_W), row(MOBA_W), row(2 * D_MODEL),
                  _const_spec((RET_V_W, D_MODEL)), _const_spec((MOBA_W, D_MODEL)),
                  _const_spec((D_MODEL, D_MODEL)), _const_spec((1, D_MODEL)), _const_spec((1, D_MODEL))],
        out_specs=row(D_MODEL),
        out_shape=jax.ShapeDtypeStruct((n, D_MODEL), _F32),
        compiler_params=_params(1),
        name="merge",
    )(x2d, ro, mo, gab, w_ret_out, w_moba_out, w_o, ln_w, ln_b)


FFN_COL_CHUNK = D_FF
CONV_PAD = 8


def _ffn_kernel(*refs, alpha, tm, seq, within_seq):
    if within_seq:
        (h_ref, wup_ref, cw_ref, cb_ref, wdn_ref, lnw_ref, lnb_ref, y_ref, cs_ref, ubuf) = refs
    else:
        (h_ref, p1_ref, p2_ref, wup_ref, cw_ref, cb_ref, wdn_ref, lnw_ref, lnb_ref, y_ref, ua_ref, ubuf) = refs
    h = h_ref[...]
    hb = h.astype(_BF)
    hist = CONV_W - 1

    if within_seq:
        tiles_per_seq = seq // tm

        @pl.when(pl.program_id(0) % tiles_per_seq == 0)
        def _():
            ubuf[0:CONV_PAD, :] = jnp.zeros((CONV_PAD, D_FF), _F32)
    else:
        t_in_seq = lax.broadcasted_iota(jnp.int32, (tm, 1), 0) % seq

        @pl.when(pl.program_id(0) == 0)
        def _():
            ubuf[0:CONV_PAD, :] = jnp.zeros((CONV_PAD, D_FF), _F32)

    f = jnp.zeros((tm, D_MODEL), _F32)
    for c0 in range(0, D_FF, FFN_COL_CHUNK):
        cols = slice(c0, c0 + FFN_COL_CHUNK)
        ua = _nn(hb, wup_ref[:, c0:c0 + FFN_COL_CHUNK])
        ug = _nn(hb, wup_ref[:, D_FF + c0:D_FF + c0 + FFN_COL_CHUNK])
        ubuf[CONV_PAD:CONV_PAD + tm, cols] = ua
        sh1 = ubuf[CONV_PAD - 1:CONV_PAD - 1 + tm, cols]
        sh2 = ubuf[CONV_PAD - 2:CONV_PAD - 2 + tm, cols]
        if not within_seq:
            sh1 = jnp.where(t_in_seq >= 1, sh1, p1_ref[:, cols])
            sh2 = jnp.where(t_in_seq >= 2, sh2, p2_ref[:, cols])
            ua_ref[:, cols] = ua
        uc = cb_ref[:, cols] + (sh2 * cw_ref[0:1, cols] + sh1 * cw_ref[1:2, cols] + ua * cw_ref[2:3, cols])
        act = (uc * jax.nn.sigmoid(uc) * ug).astype(_BF)
        f = f + _nn(act, wdn_ref[c0:c0 + FFN_COL_CHUNK, :])

    if within_seq:
        last = ubuf[CONV_PAD + tm - hist:CONV_PAD + tm, :]
        cs_ref[0] = last
        ubuf[CONV_PAD - hist:CONV_PAD, :] = last
    y_ref[...] = _layer_norm(alpha * h + f, lnw_ref[...], lnb_ref[...])


def _ffn_prompt(h2d, w_up, conv_w, conv_b, w_down, ln_w, ln_b, alpha, batch, seq, tm):
    n = h2d.shape[0]
    assert seq % tm == 0 and tm >= CONV_W - 1
    tiles_per_seq = seq // tm
    row = pl.BlockSpec((tm, D_MODEL), lambda i: (i, 0))
    return pl.pallas_call(
        functools.partial(_ffn_kernel, alpha=alpha, tm=tm, seq=seq, within_seq=True),
        grid=(n // tm,),
        in_specs=[row, _const_spec((D_MODEL, 2 * D_FF)), _const_spec((CONV_W, D_FF)), _const_spec((1, D_FF)),
                  _const_spec((D_FF, D_MODEL)), _const_spec((1, D_MODEL)), _const_spec((1, D_MODEL))],
        out_specs=[row, pl.BlockSpec((1, CONV_W - 1, D_FF), lambda i: (i // tiles_per_seq, 0, 0))],
        out_shape=[jax.ShapeDtypeStruct((n, D_MODEL), _F32),
                   jax.ShapeDtypeStruct((batch, CONV_W - 1, D_FF), _F32)],
        scratch_shapes=[pltpu.VMEM((CONV_PAD + tm, D_FF), _F32)],
        compiler_params=_params(1),
        name="ffn_prompt",
    )(h2d, w_up, conv_w, conv_b, w_down, ln_w, ln_b)


def _ffn_sample(h2d, prev1, prev2, w_up, conv_w, conv_b, w_down, ln_w, ln_b, alpha, seq, tm):
    n = h2d.shape[0]
    assert n % tm == 0 and tm % seq == 0
    row = pl.BlockSpec((tm, D_MODEL), lambda i: (i, 0))
    wide = pl.BlockSpec((tm, D_FF), lambda i: (i, 0))
    return pl.pallas_call(
        functools.partial(_ffn_kernel, alpha=alpha, tm=tm, seq=seq, within_seq=False),
        grid=(n // tm,),
        in_specs=[row, wide, wide, _const_spec((D_MODEL, 2 * D_FF)), _const_spec((CONV_W, D_FF)),
                  _const_spec((1, D_FF)), _const_spec((D_FF, D_MODEL)), _const_spec((1, D_MODEL)),
                  _const_spec((1, D_MODEL))],
        out_specs=[row, wide],
        out_shape=[jax.ShapeDtypeStruct((n, D_MODEL), _F32), jax.ShapeDtypeStruct((n, D_FF), _F32)],
        scratch_shapes=[pltpu.VMEM((CONV_PAD + tm, D_FF), _F32)],
        compiler_params=_params(1),
        name="ffn_sample",
    )(h2d, prev1, prev2, w_up, conv_w, conv_b, w_down, ln_w, ln_b)


def _pick_tile(n, cap):
    t = cap
    while n % t:
        t //= 2
    return t


def _layer(x, ret_state, conv_state, ret_chunk, ret_tiling, attend, wts, alpha, inter_dtype, row_tile):
    (w_in, gn_w, w_ret_out, w_moba_out, w_o, ln1_w, ln1_b, w_up, conv_w, conv_b, w_down, ln2_w, ln2_b) = wts
    batch, seq, _ = x.shape
    n = batch * seq
    x2d = x.reshape(n, D_MODEL)
    tm = _pick_tile(n, row_tile)
    time_minor = conv_state is None
    rq, rk, rv, rg, mq, mk, mv, gab = _in_proj(x2d, w_in, inter_dtype, tm, seq if time_minor else None)
    ro, ret_new = _retention(rq, rk, rv, rg, ret_state, gn_w, batch, seq, ret_chunk, inter_dtype, *ret_tiling)
    mo = attend(mq, mk, mv)
    h = _merge(x2d, ro, mo, gab, w_ret_out, w_moba_out, w_o, ln1_w, ln1_b, alpha, tm)
    if conv_state is None:
        y, conv_new = _ffn_prompt(h, w_up, conv_w, conv_b, w_down, ln2_w, ln2_b, alpha, batch, seq,
                                  _pick_tile(seq, row_tile))
    else:
        hist = CONV_W - 1
        prev1 = jnp.zeros((batch, seq, D_FF), _F32).at[:, 0].set(conv_state[:, hist - 1])
        prev2 = jnp.zeros((batch, seq, D_FF), _F32).at[:, 0].set(conv_state[:, hist - 2])
        prev2 = prev2.at[:, 1].set(conv_state[:, hist - 1])
        y, ua = _ffn_sample(h, prev1.reshape(n, D_FF), prev2.reshape(n, D_FF), w_up, conv_w, conv_b, w_down,
                            ln2_w, ln2_b, alpha, seq, tm)
        conv_new = jnp.concatenate([conv_state, ua.reshape(batch, seq, D_FF)], axis=1)[:, -hist:]
    if time_minor:
        unpack = lambda a: a.reshape(batch, MOBA_HEADS, MOBA_DH, seq).transpose(0, 3, 1, 2)
    else:
        unpack = lambda a: a.reshape(batch, seq, MOBA_HEADS, MOBA_DH)
    return y.reshape(batch, seq, D_MODEL), ret_new, conv_new, unpack(mk), unpack(mv)


def kernel(x_prompt, x_sample, cache_k, cache_v, page_table, state_ret, state_conv, w_in, ret_gn_w, w_ret_out,
           w_moba_out, w_o, ln1_w, ln1_b, w_up, conv_w, conv_b, w_down, ln2_w, ln2_b):
    depth = w_in.shape[0]
    alpha = (2 * depth) ** 0.25
    batch, seq, _ = x_prompt.shape
    dec_batch, dec_seq, _ = x_sample.shape
    n_phys, page = cache_k.shape[1], cache_k.shape[2]
    slope_row = jnp.repeat(jnp.exp2(-8.0 * (jnp.arange(MOBA_HEADS, dtype=_F32) + 1.0) / MOBA_HEADS),
                           MOBA_DH)[None, :]
    hp, hs = x_prompt, x_sample
    outs = [[] for _ in range(8)]
    for l in range(depth):
        wts = (w_in[l].astype(_BF), ret_gn_w[l][None, :], w_ret_out[l].astype(_BF), w_moba_out[l].astype(_BF),
               w_o[l].astype(_BF), ln1_w[l][None, :], ln1_b[l][None, :], w_up[l].astype(_BF), conv_w[l],
               conv_b[l][None, :], w_down[l].astype(_BF), ln2_w[l][None, :], ln2_b[l][None, :])
        attend_p = functools.partial(_moba_prompt, slope_row=slope_row, batch=batch, seq=seq, out_dtype=_BF)
        hp, rp, cp, kp, vp = _layer(hp, jnp.zeros((batch, RET_HEADS, RET_DK, RET_DV), _F32), None, RET_CHUNK,
                                    (1, 1), attend_p, wts, alpha, _BF, 512)
        ck = jnp.transpose(cache_k[l], (0, 2, 3, 1)).reshape(n_phys, MOBA_W, page)
        cv = jnp.transpose(cache_v[l], (0, 2, 3, 1)).reshape(n_phys, MOBA_W, page)
        attend_s = functools.partial(_moba_sample, slope_row=slope_row, cache_kt=ck, cache_vt=cv,
                                     page_table=page_table, t_new=dec_seq, out_dtype=_F32)
        hs, rs, cs, ks_, vs_ = _layer(hs, state_ret[l], state_conv[l], dec_seq, (RET_HEADS, _pick_tile(dec_batch, 8)),
                                      attend_s, wts, alpha, _F32, 128)
        for lst, val in zip(outs, (kp, vp, rp, cp, ks_, vs_, rs, cs)):
            lst.append(val)
    return (hp, hs) + tuple(jnp.stack(o) for o in outs)
```

```python
import functools

import jax
import jax.numpy as jnp
from jax import lax
from jax.experimental import pallas as pl
from jax.experimental.pallas import tpu as pltpu

D_MODEL = 1024
RET_HEADS = 4
RET_DK = 128
RET_DV = 256
RET_CHUNK = 256
MOBA_HEADS = 8
MOBA_DH = 64
MOBA_BLOCK = 256
MOBA_TOPK = 3
D_FF = 2816
CONV_W = 3
LN_EPS = 1e-5
GN_EPS = 1e-6

RET_QK_W = RET_HEADS * RET_DK
RET_V_W = RET_HEADS * RET_DV
MOBA_W = MOBA_HEADS * MOBA_DH
IN_SPLITS = (RET_QK_W, RET_QK_W, RET_V_W, RET_V_W, MOBA_W, MOBA_W, MOBA_W, D_MODEL, D_MODEL)
N_IN = sum(IN_SPLITS)

LANES = 128
VMEM_LIMIT_BYTES = 56 * 1024 * 1024
NEG_BIG = -1e30
LOG2E = 1.4426950408889634
ONES_ROWS = 16

_BF = jnp.bfloat16
_F32 = jnp.float32


def _nt(a, b):
    return lax.dot_general(a, b, (((1,), (1,)), ((), ())), preferred_element_type=_F32)


def _nn(a, b):
    return jnp.dot(a, b, preferred_element_type=_F32)


def _split_bf16(x):
    hi = x.astype(_BF)
    lo = (x - hi.astype(_F32)).astype(_BF)
    return hi, lo


def _nt3(a, b):
    ah, al = _split_bf16(a)
    bh, bl = _split_bf16(b)
    return _nt(ah, bh) + (_nt(al, bh) + _nt(ah, bl))


def _layer_norm(x, w, b):
    mu = jnp.mean(x, axis=-1, keepdims=True)
    d = x - mu
    var = jnp.mean(d * d, axis=-1, keepdims=True)
    return d * lax.rsqrt(var + LN_EPS) * w + b


def _const_spec(shape):
    return pl.BlockSpec(shape, lambda *_: (0,) * len(shape), pipeline_mode=pl.Buffered(1))


def _params(n_grid):
    return pltpu.CompilerParams(dimension_semantics=("arbitrary",) * n_grid,
                                vmem_limit_bytes=VMEM_LIMIT_BYTES)


def _in_proj_kernel(x_ref, w_ref, rq_ref, rk_ref, rv_ref, rg_ref, mq_ref, mk_ref, mv_ref, gab_ref):
    xb = x_ref[...].astype(_BF)
    offs = [0]
    for w in IN_SPLITS:
        offs.append(offs[-1] + w)

    def seg(i, j=None):
        return _nn(xb, w_ref[:, offs[i]:offs[(i if j is None else j) + 1]])

    rq_ref[...] = seg(0).astype(rq_ref.dtype)
    rk_ref[...] = (seg(1) * RET_DK ** -0.5).astype(rk_ref.dtype)
    rv_ref[...] = seg(2).astype(rv_ref.dtype)
    rg_ref[...] = seg(3).astype(rg_ref.dtype)
    mq_ref[...] = seg(4)
    if len(mk_ref.shape) == 3:
        mk_ref[0] = seg(5).T
        mv_ref[0] = seg(6).T
    else:
        mk_ref[...] = seg(5)
        mv_ref[...] = seg(6)
    gab_ref[...] = seg(7, 8).astype(gab_ref.dtype)


def _in_proj(x2d, w_in_bf, inter_dtype, tm, kv_time_minor_seq=None):
    n = x2d.shape[0]
    assert n % tm == 0
    widths = (RET_QK_W, RET_QK_W, RET_V_W, RET_V_W, MOBA_W, MOBA_W, MOBA_W, 2 * D_MODEL)
    dtypes = (inter_dtype,) * 4 + (_F32,) * 3 + (inter_dtype,)
    out_specs = [pl.BlockSpec((tm, w), lambda i: (i, 0)) for w in widths]
    out_shape = [jax.ShapeDtypeStruct((n, w), dt) for w, dt in zip(widths, dtypes)]
    if kv_time_minor_seq is not None:
        seq = kv_time_minor_seq
        assert seq % tm == 0 and tm % LANES == 0
        tps = seq // tm
        for j in (5, 6):
            out_specs[j] = pl.BlockSpec((1, MOBA_W, tm), lambda i: (i // tps, 0, i % tps))
            out_shape[j] = jax.ShapeDtypeStruct((n // seq, MOBA_W, seq), _F32)
    return pl.pallas_call(
        _in_proj_kernel,
        grid=(n // tm,),
        in_specs=[pl.BlockSpec((tm, D_MODEL), lambda i: (i, 0)), _const_spec((D_MODEL, N_IN))],
        out_specs=out_specs,
        out_shape=out_shape,
        compiler_params=_params(1),
        name="in_proj",
    )(x2d, w_in_bf)


def _retention_kernel(q_ref, k_ref, v_ref, g_ref, s0_ref, intra_ref, qdec_ref, kdec_ref, cdec_ref,
                      gnw_ref, ro_ref, sn_ref, *, seq, chunk, cpad, n_heads, n_seqs):
    n_chunks = seq // chunk

    def pad(a):
        if chunk == cpad:
            return a
        return jnp.concatenate([a, jnp.zeros((cpad - chunk, a.shape[1]), a.dtype)], axis=0)

    for h in range(n_heads):
        intra = intra_ref[h]
        qdec = qdec_ref[h]
        kdec = kdec_ref[h]
        cdec = cdec_ref[h]
        kcols = slice(h * RET_DK, (h + 1) * RET_DK)
        vcols = slice(h * RET_DV, (h + 1) * RET_DV)
        gnw = gnw_ref[:, vcols]
        for bb in range(n_seqs):
            s = s0_ref[bb, h]
            for c in range(n_chunks):
                rows = slice(bb * seq + c * chunk, bb * seq + (c + 1) * chunk)
                q = pad(q_ref[rows, kcols].astype(_F32))
                k = pad(k_ref[rows, kcols].astype(_F32))
                v = pad(v_ref[rows, vcols].astype(_F32)).astype(_BF)
                a = _nt(q.astype(_BF), k.astype(_BF)) * intra
                o = _nn(a.astype(_BF), v) + _nn((q * qdec).astype(_BF), s.astype(_BF))
                s = cdec * s + _nn((k * kdec).T.astype(_BF), v)
                o = o[:chunk]
                mu = jnp.mean(o, axis=-1, keepdims=True)
                d = o - mu
                var = jnp.mean(d * d, axis=-1, keepdims=True)
                on = d * lax.rsqrt(var + GN_EPS) * gnw
                g = g_ref[rows, vcols].astype(_F32)
                ro_ref[rows, vcols] = (g * jax.nn.sigmoid(g) * on).astype(ro_ref.dtype)
            sn_ref[bb, h] = s


def _decay_tables(chunk, cpad):
    log_g = jnp.log1p(-jnp.exp2(-5.0 - jnp.arange(RET_HEADS, dtype=_F32)))
    i = jnp.arange(chunk, dtype=_F32)
    diff = i[:, None] - i[None, :]
    intra = jnp.where(diff >= 0, jnp.exp(jnp.maximum(diff, 0.0)[None] * log_g[:, None, None]), 0.0)
    q_dec = jnp.exp((i[None] + 1.0) * log_g[:, None])
    k_dec = jnp.exp((chunk - 1.0 - i[None]) * log_g[:, None])
    c_dec = jnp.exp(chunk * log_g)
    p = cpad - chunk
    intra = jnp.pad(intra, ((0, 0), (0, p), (0, p)))
    q_dec = jnp.broadcast_to(jnp.pad(q_dec, ((0, 0), (0, p)))[:, :, None], (RET_HEADS, cpad, RET_DK))
    k_dec = jnp.broadcast_to(jnp.pad(k_dec, ((0, 0), (0, p)))[:, :, None], (RET_HEADS, cpad, RET_DK))
    c_dec = jnp.broadcast_to(c_dec[:, None, None], (RET_HEADS, 1, RET_DV))
    return intra, q_dec, k_dec, c_dec


def _retention(rq, rk, rv, rg, s0, gn_w, batch, seq, chunk, out_dtype, n_heads, n_seqs):
    cpad = max(chunk, LANES)
    assert seq % chunk == 0 and batch % n_seqs == 0 and RET_HEADS % n_heads == 0
    intra, q_dec, k_dec, c_dec = _decay_tables(chunk, cpad)
    kern = functools.partial(_retention_kernel, seq=seq, chunk=chunk, cpad=cpad, n_heads=n_heads, n_seqs=n_seqs)
    head_tab = lambda shape: pl.BlockSpec((n_heads,) + shape, lambda b, h: (h, 0, 0))
    tok = lambda w: pl.BlockSpec((n_seqs * seq, n_heads * w), lambda b, h: (b, h))
    state = pl.BlockSpec((n_seqs, n_heads, RET_DK, RET_DV), lambda b, h: (b, h, 0, 0))
    return pl.pallas_call(
        kern,
        grid=(batch // n_seqs, RET_HEADS // n_heads),
        in_specs=[tok(RET_DK), tok(RET_DK), tok(RET_DV), tok(RET_DV), state,
                  head_tab((cpad, cpad)), head_tab((cpad, RET_DK)), head_tab((cpad, RET_DK)),
                  head_tab((1, RET_DV)),
                  pl.BlockSpec((1, n_heads * RET_DV), lambda b, h: (0, h))],
        out_specs=[tok(RET_DV), state],
        out_shape=[
            jax.ShapeDtypeStruct((batch * seq, RET_V_W), out_dtype),
            jax.ShapeDtypeStruct((batch, RET_HEADS, RET_DK, RET_DV), _F32),
        ],
        compiler_params=_params(2),
        name="retention",
    )(rq, rk, rv, rg, s0, intra, q_dec, k_dec, c_dec, gn_w)


def _split3_bf16(x):
    h = x.astype(_BF)
    r = x - h.astype(_F32)
    m = r.astype(_BF)
    l = (r - m.astype(_F32)).astype(_BF)
    return h, m, l


def _moba_prompt_kernel(q_ref, kt_ref, vt_ref, slope_ref, o_ref, *, seq):
    blk = MOBA_BLOCK
    nb = seq // blk
    hw = MOBA_DH
    scale = MOBA_DH ** -0.5
    q = q_ref[...] * (scale * LOG2E)
    k = kt_ref[0].T
    vt = jnp.concatenate([vt_ref[0].astype(_BF), jnp.ones((ONES_ROWS, seq), _BF)], axis=0)
    kmean = jnp.concatenate(
        [jnp.sum(k[n * blk:(n + 1) * blk], axis=0, keepdims=True) for n in range(nb)], axis=0) * (1.0 / blk)

    k_lane = lax.broadcasted_iota(jnp.int32, (seq, LANES), 1)
    k_off = (lax.broadcasted_iota(jnp.int32, (seq, LANES), 0) % blk).astype(_F32)
    k_aug = jnp.where(k_lane < 3, k_off, jnp.where(k_lane < 6, 1.0, 0.0))
    kb = jnp.concatenate([k.astype(_BF), k_aug.astype(_BF)], axis=1)

    q_lane = lax.broadcasted_iota(jnp.int32, (2 * blk, LANES), 1)
    q_row = lax.broadcasted_iota(jnp.int32, (2 * blk, LANES), 0)
    s_col = jnp.where(q_row < blk, slope_ref[0:1, 0:1], slope_ref[0:1, hw:hw + 1]) * LOG2E
    c_col = -s_col * (q_row % blk).astype(_F32)
    s3, c3 = _split3_bf16(s_col), _split3_bf16(c_col)
    q_aug = jnp.zeros((2 * blk, LANES), _F32)
    for j in range(3):
        q_aug = jnp.where(q_lane == j, s3[j].astype(_F32), jnp.where(q_lane == 3 + j, c3[j].astype(_F32), q_aug))
    q_aug = q_aug.astype(_BF)

    lane_head = lax.broadcasted_iota(jnp.int32, (blk, 2 * hw), 1) // hw
    key_i = lax.broadcasted_iota(jnp.int32, (blk, 2 * blk), 0)
    qry_i = lax.broadcasted_iota(jnp.int32, (blk, 2 * blk), 1) % blk
    causal = qry_i >= key_i
    col_head = lax.broadcasted_iota(jnp.int32, (1, 2 * blk), 1) // blk
    slope = jnp.where(col_head == 0, slope_ref[0:1, 0:1], slope_ref[0:1, hw:hw + 1]) * LOG2E
    row_id = lax.broadcasted_iota(jnp.int32, (nb, 2 * blk), 0)

    def scores(i):
        qi = q[i * blk:(i + 1) * blk]
        qexp = jnp.concatenate([jnp.where(lane_head == 0, qi, 0.0), jnp.where(lane_head == 1, qi, 0.0)], axis=0)
        st = _nt(kb[:(i + 1) * blk], jnp.concatenate([qexp.astype(_BF), q_aug], axis=1))
        return i, qexp, st

    def row_max(i, qexp, st):
        sel = None
        if i > MOBA_TOPK:
            gt = _nt3(kmean, qexp)
            valid = row_id < i
            sel = []
            for n in range(i):
                gn = gt[n:n + 1]
                ahead = ((gt > gn) | ((gt == gn) & (row_id < n))) & valid
                rank = jnp.sum(ahead.astype(_F32), axis=0, keepdims=True)
                sel.append(rank < MOBA_TOPK)
        keep = [causal if n == i else (None if sel is None else sel[n]) for n in range(i + 1)]
        shifts = [slope * float(-(i - n) * blk) for n in range(i + 1)]
        tops = []
        for n in range(i + 1):
            t = st[n * blk:(n + 1) * blk]
            if keep[n] is not None:
                t = jnp.where(keep[n], t, NEG_BIG)
            tops.append(jnp.max(t, axis=0, keepdims=True) + shifts[n])
        m = functools.reduce(jnp.maximum, tops)
        return i, st, keep, [sh - m for sh in shifts]

    def probs(i, st, keep, offs):
        ps = []
        for n in range(i + 1):
            t = st[n * blk:(n + 1) * blk] + offs[n]
            if keep[n] is not None:
                t = jnp.where(keep[n], t, NEG_BIG)
            ps.append(jnp.exp2(t).astype(_BF))
        return i, (ps[0] if i == 0 else jnp.concatenate(ps, axis=0))

    def finish(i, pt):
        ot = _nn(vt[:, :(i + 1) * blk], pt)
        ot = ot[:2 * hw] * (1.0 / ot[2 * hw:2 * hw + 1])
        oi = jnp.concatenate([ot[0:hw, 0:blk], ot[hw:2 * hw, blk:2 * blk]], axis=0)
        o_ref[i * blk:(i + 1) * blk, :] = oi.T.astype(o_ref.dtype)

    stages = (scores, row_max, probs, finish)
    in_flight = [None] * len(stages)
    for step in range(nb + len(stages) - 1):
        for j in reversed(range(1, len(stages))):
            if in_flight[j] is not None:
                out = stages[j](*in_flight[j])
                in_flight[j] = None
                if j + 1 < len(stages):
                    in_flight[j + 1] = out
        if step < nb:
            in_flight[1] = scores(step)


def _moba_prompt(mq, mkt, mvt, slope_row, batch, seq, out_dtype):
    width = 2 * MOBA_DH
    assert width == LANES and seq % MOBA_BLOCK == 0
    kern = functools.partial(_moba_prompt_kernel, seq=seq)
    spec = pl.BlockSpec((seq, width), lambda b, h: (b, h))
    spec_t = pl.BlockSpec((1, width, seq), lambda b, h: (b, h, 0))
    return pl.pallas_call(
        kern,
        grid=(batch, MOBA_W // width),
        in_specs=[spec, spec_t, spec_t, pl.BlockSpec((1, width), lambda b, h: (0, h))],
        out_specs=spec,
        out_shape=jax.ShapeDtypeStruct((batch * seq, MOBA_W), out_dtype),
        compiler_params=_params(2),
        name="moba_prompt",
    )(mq, mkt, mvt, slope_row)


RING_SLOTS = 4
KEY_GROUPS = 4


def _moba_sample_kernel(pt_ref, q_ref, kn_ref, vn_ref, slope_ref, ck_hbm, cv_hbm, o_ref,
                        ring, s_scr, p_scr, bias_scr, sem, *, n_pages, page, t_new):
    b = pl.program_id(0)
    n_seq = pl.num_programs(0)
    blk = MOBA_BLOCK
    past = n_pages * page
    nb = past // blk
    rows = MOBA_HEADS * t_new
    scale = MOBA_DH ** -0.5
    ppc = n_pages // KEY_GROUPS
    cw = ppc * page
    bpc = cw // blk
    n_chunks = 2 * KEY_GROUPS
    lead = RING_SLOTS - 1

    def copies(seq_idx, c):
        src = ck_hbm if c < KEY_GROUPS else cv_hbm
        first = (c % KEY_GROUPS) * ppc
        slot = c % RING_SLOTS
        return [pltpu.make_async_copy(src.at[pt_ref[seq_idx, first + j]],
                                      ring.at[slot, :, pl.ds(j * page, page)], sem.at[slot])
                for j in range(ppc)]

    def start(seq_idx, c):
        for d in copies(seq_idx, c):
            d.start()

    q_rows = jnp.concatenate([q_ref[...]] * MOBA_HEADS, axis=0)
    r_head = lax.broadcasted_iota(jnp.int32, (rows, MOBA_W), 0) // t_new
    c_head = lax.broadcasted_iota(jnp.int32, (rows, MOBA_W), 1) // MOBA_DH
    own = r_head == c_head
    q_hi, q_lo = _split_bf16(jnp.where(own, q_rows, 0.0) * scale)
    q_stack = jnp.concatenate([q_hi, q_lo], axis=0)
    slope = jnp.sum(jnp.where(own, jnp.broadcast_to(slope_ref[...], (rows, MOBA_W)), 0.0),
                    axis=-1, keepdims=True) * (1.0 / MOBA_DH)
    tok = lax.broadcasted_iota(jnp.int32, (rows, blk), 0) % t_new
    off = lax.broadcasted_iota(jnp.int32, (rows, blk), 1)

    @pl.when(b == 0)
    def _():
        for c in range(lead):
            start(0, c)
        for n in range(nb):
            dist = (past - n * blk) + (tok - off)
            bias_scr[:, n * blk:(n + 1) * blk] = -slope * dist.astype(_F32)

    gate = jnp.zeros((rows, nb), _F32)
    bid = lax.broadcasted_iota(jnp.int32, (rows, nb), 1)
    acc = den = None
    for c in range(n_chunks):
        nxt = c + lead
        if nxt < n_chunks:
            start(b, nxt)
        else:
            @pl.when(b + 1 < n_seq)
            def _():
                start(b + 1, nxt - n_chunks)
        for d in copies(b, c):
            d.wait()
        slot = c % RING_SLOTS
        if c < KEY_GROUPS:
            kc = ring[slot]
            k_hi, k_lo = _split_bf16(kc)
            r1 = _nn(q_stack, k_hi)
            s_scr[:, c * cw:(c + 1) * cw] = r1[:rows]
            g = r1[:rows] + (r1[rows:] + _nn(q_hi, k_lo))
            for j in range(bpc):
                gs = jnp.sum(g[:, j * blk:(j + 1) * blk], axis=-1, keepdims=True)
                gate = jnp.where(bid == c * bpc + j, gs, gate)
        else:
            cc = c - KEY_GROUPS
            acc = acc + _nt(p_scr[:, cc * cw:(cc + 1) * cw], ring[slot].astype(_BF))
        if c == KEY_GROUPS - 1:
            sel = jnp.zeros((rows, nb), jnp.bool_)
            for _ in range(min(MOBA_TOPK, nb)):
                mx = jnp.max(gate, axis=-1, keepdims=True)
                first = jnp.min(jnp.where(gate == mx, bid, nb), axis=-1, keepdims=True)
                hit = bid == first
                sel = sel | hit
                gate = jnp.where(hit, -jnp.inf, gate)
            zpad = jnp.zeros((page - t_new, MOBA_W), _F32)
            kn = jnp.concatenate([kn_ref[...], zpad], axis=0).astype(_BF)
            vn = jnp.concatenate([vn_ref[...], zpad], axis=0).astype(_BF)
            tok_p = lax.broadcasted_iota(jnp.int32, (rows, page), 0) % t_new
            off_p = lax.broadcasted_iota(jnp.int32, (rows, page), 1)
            s_own = _nt(q_hi, kn) - slope * (tok_p - off_p).astype(_F32)
            s_own = jnp.where(off_p <= tok_p, s_own, NEG_BIG)
            mrun = jnp.full((rows, blk), NEG_BIG, _F32)
            for n in range(nb):
                cols = slice(n * blk, (n + 1) * blk)
                t = jnp.where(sel[:, n:n + 1], s_scr[:, cols] + bias_scr[:, cols], NEG_BIG)
                s_scr[:, cols] = t
                mrun = jnp.maximum(mrun, t)
            m = jnp.maximum(jnp.max(mrun, axis=-1, keepdims=True), jnp.max(s_own, axis=-1, keepdims=True))
            p_own = jnp.exp(s_own - m)
            lrun = jnp.zeros((rows, blk), _F32)
            for n in range(nb):
                cols = slice(n * blk, (n + 1) * blk)
                e = jnp.exp(s_scr[:, cols] - m)
                lrun = lrun + e
                p_scr[:, cols] = e.astype(_BF)
            den = jnp.sum(lrun, axis=-1, keepdims=True) + jnp.sum(p_own, axis=-1, keepdims=True)
            acc = _nn(p_own.astype(_BF), vn)

    out = acc * (1.0 / den)
    o_ref[...] = jnp.concatenate(
        [out[h * t_new:(h + 1) * t_new, h * MOBA_DH:(h + 1) * MOBA_DH] for h in range(MOBA_HEADS)],
        axis=1).astype(o_ref.dtype)


def _moba_sample(mq, mk, mv, slope_row, cache_kt, cache_vt, page_table, t_new, out_dtype):
    batch, n_pages = page_table.shape
    page = cache_kt.shape[2]
    assert page == LANES and t_new <= page and MOBA_BLOCK % page == 0
    assert n_pages % KEY_GROUPS == 0 and (n_pages // KEY_GROUPS * page) % MOBA_BLOCK == 0
    assert (2 * KEY_GROUPS) % RING_SLOTS == 0
    past = n_pages * page
    cw = n_pages // KEY_GROUPS * page
    rows = MOBA_HEADS * t_new
    kern = functools.partial(_moba_sample_kernel, n_pages=n_pages, page=page, t_new=t_new)
    tok = pl.BlockSpec((t_new, MOBA_W), lambda b, pt: (b, 0))
    grid_spec = pltpu.PrefetchScalarGridSpec(
        num_scalar_prefetch=1,
        grid=(batch,),
        in_specs=[tok, tok, tok, pl.BlockSpec((1, MOBA_W), lambda b, pt: (0, 0)),
                  pl.BlockSpec(memory_space=pl.ANY), pl.BlockSpec(memory_space=pl.ANY)],
        out_specs=tok,
        scratch_shapes=[
            pltpu.VMEM((RING_SLOTS, MOBA_W, cw), _F32),
            pltpu.VMEM((rows, past), _F32),
            pltpu.VMEM((rows, past), _BF),
            pltpu.VMEM((rows, past), _F32),
            pltpu.SemaphoreType.DMA((RING_SLOTS,)),
        ],
    )
    return pl.pallas_call(
        kern,
        grid_spec=grid_spec,
        out_shape=jax.ShapeDtypeStruct((batch * t_new, MOBA_W), out_dtype),
        compiler_params=_params(1),
        name="moba_sample",
    )(page_table, mq, mk, mv, slope_row, cache_kt, cache_vt)


def _merge_kernel(x_ref, ro_ref, mo_ref, gab_ref, wr_ref, wm_ref, wo_ref, lnw_ref, lnb_ref, h_ref, *, alpha):
    a = _nn(ro_ref[...].astype(_BF), wr_ref[...])
    m = _nn(mo_ref[...].astype(_BF), wm_ref[...])
    ga = gab_ref[:, :D_MODEL].astype(_F32)
    gb = gab_ref[:, D_MODEL:].astype(_F32)
    merged = jax.nn.sigmoid(ga) * a + jax.nn.sigmoid(gb) * m
    pre = alpha * x_ref[...] + _nn(merged.astype(_BF), wo_ref[...])
    h_ref[...] = _layer_norm(pre, lnw_ref[...], lnb_ref[...])


def _merge(x2d, ro, mo, gab, w_ret_out, w_moba_out, w_o, ln_w, ln_b, alpha, tm):
    n = x2d.shape[0]
    assert n % tm == 0
    row = lambda w: pl.BlockSpec((tm, w), lambda i: (i, 0))
    return pl.pallas_call(
        functools.partial(_merge_kernel, alpha=alpha),
        grid=(n // tm,),
        in_specs=[row(D_MODEL), row(RET_V_W), row(MOBA_W), row(2 * D_MODEL),
                  _const_spec((RET_V_W, D_MODEL)), _const_spec((MOBA_W, D_MODEL)),
                  _const_spec((D_MODEL, D_MODEL)), _const_spec((1, D_MODEL)), _const_spec((1, D_MODEL))],
        out_specs=row(D_MODEL),
        out_shape=jax.ShapeDtypeStruct((n, D_MODEL), _F32),
        compiler_params=_params(1),
        name="merge",
    )(x2d, ro, mo, gab, w_ret_out, w_moba_out, w_o, ln_w, ln_b)


FFN_COL_CHUNK = D_FF
CONV_PAD = 8
FUSED_ROW_TILE = 512
FUSED_SUBTILES = 2


def _ffn_kernel(*refs, alpha, tm, seq, within_seq):
    if within_seq:
        (h_ref, wup_ref, cw_ref, cb_ref, wdn_ref, lnw_ref, lnb_ref, y_ref, cs_ref, ubuf) = refs
    else:
        (h_ref, p1_ref, p2_ref, wup_ref, cw_ref, cb_ref, wdn_ref, lnw_ref, lnb_ref, y_ref, ua_ref, ubuf) = refs
    h = h_ref[...]
    hb = h.astype(_BF)
    hist = CONV_W - 1

    if within_seq:
        tiles_per_seq = seq // tm

        @pl.when(pl.program_id(0) % tiles_per_seq == 0)
        def _():
            ubuf[0:CONV_PAD, :] = jnp.zeros((CONV_PAD, D_FF), _F32)
    else:
        t_in_seq = lax.broadcasted_iota(jnp.int32, (tm, 1), 0) % seq

        @pl.when(pl.program_id(0) == 0)
        def _():
            ubuf[0:CONV_PAD, :] = jnp.zeros((CONV_PAD, D_FF), _F32)

    f = jnp.zeros((tm, D_MODEL), _F32)
    for c0 in range(0, D_FF, FFN_COL_CHUNK):
        cols = slice(c0, c0 + FFN_COL_CHUNK)
        ua = _nn(hb, wup_ref[:, c0:c0 + FFN_COL_CHUNK])
        ug = _nn(hb, wup_ref[:, D_FF + c0:D_FF + c0 + FFN_COL_CHUNK])
        ubuf[CONV_PAD:CONV_PAD + tm, cols] = ua
        sh1 = ubuf[CONV_PAD - 1:CONV_PAD - 1 + tm, cols]
        sh2 = ubuf[CONV_PAD - 2:CONV_PAD - 2 + tm, cols]
        if not within_seq:
            sh1 = jnp.where(t_in_seq >= 1, sh1, p1_ref[:, cols])
            sh2 = jnp.where(t_in_seq >= 2, sh2, p2_ref[:, cols])
            ua_ref[:, cols] = ua
        uc = cb_ref[:, cols] + (sh2 * cw_ref[0:1, cols] + sh1 * cw_ref[1:2, cols] + ua * cw_ref[2:3, cols])
        act = (uc * jax.nn.sigmoid(uc) * ug).astype(_BF)
        f = f + _nn(act, wdn_ref[c0:c0 + FFN_COL_CHUNK, :])

    if within_seq:
        last = ubuf[CONV_PAD + tm - hist:CONV_PAD + tm, :]
        cs_ref[0] = last
        ubuf[CONV_PAD - hist:CONV_PAD, :] = last
    y_ref[...] = _layer_norm(alpha * h + f, lnw_ref[...], lnb_ref[...])


def _merge_ffn_kernel(x_ref, ro_ref, mo_ref, gab_ref, wr_ref, wm_ref, wo_ref, l1w_ref, l1b_ref,
                      wup_ref, cw_ref, cb_ref, wdn_ref, l2w_ref, l2b_ref, y_ref, cs_ref, ubuf, *, alpha, tm, seq):
    hist = CONV_W - 1
    sub = tm // FUSED_SUBTILES

    @pl.when(pl.program_id(0) % (seq // tm) == 0)
    def _():
        ubuf[0:CONV_PAD, :] = jnp.zeros((CONV_PAD, D_FF), _F32)

    def mix_mm(r):
        rows = slice(r * sub, (r + 1) * sub)
        return r, _nn(ro_ref[rows, :].astype(_BF), wr_ref[...]), _nn(mo_ref[rows, :].astype(_BF), wm_ref[...])

    def gate(r, a, m):
        rows = slice(r * sub, (r + 1) * sub)
        ga = gab_ref[rows, :D_MODEL].astype(_F32)
        gb = gab_ref[rows, D_MODEL:].astype(_F32)
        return r, (jax.nn.sigmoid(ga) * a + jax.nn.sigmoid(gb) * m).astype(_BF)

    def out_mm(r, merged):
        return r, _nn(merged, wo_ref[...])

    def norm1(r, proj):
        rows = slice(r * sub, (r + 1) * sub)
        h = _layer_norm(alpha * x_ref[rows, :] + proj, l1w_ref[...], l1b_ref[...])
        return r, h, h.astype(_BF)

    def up_mm(r, h, hb):
        ua = _nn(hb, wup_ref[:, :D_FF])
        ug = _nn(hb, wup_ref[:, D_FF:])
        ubuf[CONV_PAD + r * sub:CONV_PAD + (r + 1) * sub, :] = ua
        return r, h, ua, ug

    def conv_act(r, h, ua, ug):
        lo = CONV_PAD + r * sub
        sh1 = ubuf[lo - 1:lo - 1 + sub, :]
        sh2 = ubuf[lo - 2:lo - 2 + sub, :]
        uc = cb_ref[...] + (sh2 * cw_ref[0:1, :] + sh1 * cw_ref[1:2, :] + ua * cw_ref[2:3, :])
        return r, h, (uc * jax.nn.sigmoid(uc) * ug).astype(_BF)

    def down_mm(r, h, act):
        return r, h, _nn(act, wdn_ref[...])

    def norm2(r, h, f):
        y_ref[r * sub:(r + 1) * sub, :] = _layer_norm(alpha * h + f, l2w_ref[...], l2b_ref[...])

    stages = (mix_mm, gate, out_mm, norm1, up_mm, conv_act, down_mm, norm2)
    in_flight = [None] * len(stages)
    for step in range(FUSED_SUBTILES + len(stages) - 1):
        for j in reversed(range(1, len(stages))):
            if in_flight[j] is not None:
                out = stages[j](*in_flight[j])
                in_flight[j] = None
                if j + 1 < len(stages):
                    in_flight[j + 1] = out
        if step < FUSED_SUBTILES:
            in_flight[1] = mix_mm(step)

    last = ubuf[CONV_PAD + tm - hist:CONV_PAD + tm, :]
    cs_ref[0] = last
    ubuf[CONV_PAD - hist:CONV_PAD, :] = last


def _merge_ffn_prompt(x2d, ro, mo, gab, wts, alpha, batch, seq, tm):
    (_, _, w_ret_out, w_moba_out, w_o, ln1_w, ln1_b, w_up, conv_w, conv_b, w_down, ln2_w, ln2_b) = wts
    n = x2d.shape[0]
    assert seq % tm == 0 and tm >= CONV_W - 1
    tiles_per_seq = seq // tm
    row = lambda w: pl.BlockSpec((tm, w), lambda i: (i, 0))
    return pl.pallas_call(
        functools.partial(_merge_ffn_kernel, alpha=alpha, tm=tm, seq=seq),
        grid=(n // tm,),
        in_specs=[row(D_MODEL), row(RET_V_W), row(MOBA_W), row(2 * D_MODEL),
                  _const_spec((RET_V_W, D_MODEL)), _const_spec((MOBA_W, D_MODEL)), _const_spec((D_MODEL, D_MODEL)),
                  _const_spec((1, D_MODEL)), _const_spec((1, D_MODEL)),
                  _const_spec((D_MODEL, 2 * D_FF)), _const_spec((CONV_W, D_FF)), _const_spec((1, D_FF)),
                  _const_spec((D_FF, D_MODEL)), _const_spec((1, D_MODEL)), _const_spec((1, D_MODEL))],
        out_specs=[row(D_MODEL), pl.BlockSpec((1, CONV_W - 1, D_FF), lambda i: (i // tiles_per_seq, 0, 0))],
        out_shape=[jax.ShapeDtypeStruct((n, D_MODEL), _F32),
                   jax.ShapeDtypeStruct((batch, CONV_W - 1, D_FF), _F32)],
        scratch_shapes=[pltpu.VMEM((CONV_PAD + tm, D_FF), _F32)],
        compiler_params=_params(1),
        name="merge_ffn",
    )(x2d, ro, mo, gab, w_ret_out, w_moba_out, w_o, ln1_w, ln1_b, w_up, conv_w, conv_b, w_down, ln2_w, ln2_b)


def _ffn_prompt(h2d, w_up, conv_w, conv_b, w_down, ln_w, ln_b, alpha, batch, seq, tm):
    n = h2d.shape[0]
    assert seq % tm == 0 and tm >= CONV_W - 1
    tiles_per_seq = seq // tm
    row = pl.BlockSpec((tm, D_MODEL), lambda i: (i, 0))
    return pl.pallas_call(
        functools.partial(_ffn_kernel, alpha=alpha, tm=tm, seq=seq, within_seq=True),
        grid=(n // tm,),
        in_specs=[row, _const_spec((D_MODEL, 2 * D_FF)), _const_spec((CONV_W, D_FF)), _const_spec((1, D_FF)),
                  _const_spec((D_FF, D_MODEL)), _const_spec((1, D_MODEL)), _const_spec((1, D_MODEL))],
        out_specs=[row, pl.BlockSpec((1, CONV_W - 1, D_FF), lambda i: (i // tiles_per_seq, 0, 0))],
        out_shape=[jax.ShapeDtypeStruct((n, D_MODEL), _F32),
                   jax.ShapeDtypeStruct((batch, CONV_W - 1, D_FF), _F32)],
        scratch_shapes=[pltpu.VMEM((CONV_PAD + tm, D_FF), _F32)],
        compiler_params=_params(1),
        name="ffn_prompt",
    )(h2d, w_up, conv_w, conv_b, w_down, ln_w, ln_b)


def _ffn_sample(h2d, prev1, prev2, w_up, conv_w, conv_b, w_down, ln_w, ln_b, alpha, seq, tm):
    n = h2d.shape[0]
    assert n % tm == 0 and tm % seq == 0
    row = pl.BlockSpec((tm, D_MODEL), lambda i: (i, 0))
    wide = pl.BlockSpec((tm, D_FF), lambda i: (i, 0))
    return pl.pallas_call(
        functools.partial(_ffn_kernel, alpha=alpha, tm=tm, seq=seq, within_seq=False),
        grid=(n // tm,),
        in_specs=[row, wide, wide, _const_spec((D_MODEL, 2 * D_FF)), _const_spec((CONV_W, D_FF)),
                  _const_spec((1, D_FF)), _const_spec((D_FF, D_MODEL)), _const_spec((1, D_MODEL)),
                  _const_spec((1, D_MODEL))],
        out_specs=[row, wide],
        out_shape=[jax.ShapeDtypeStruct((n, D_MODEL), _F32), jax.ShapeDtypeStruct((n, D_FF), _F32)],
        scratch_shapes=[pltpu.VMEM((CONV_PAD + tm, D_FF), _F32)],
        compiler_params=_params(1),
        name="ffn_sample",
    )(h2d, prev1, prev2, w_up, conv_w, conv_b, w_down, ln_w, ln_b)


def _pick_tile(n, cap):
    t = cap
    while n % t:
        t //= 2
    return t


def _layer(x, ret_state, conv_state, ret_chunk, ret_tiling, attend, wts, alpha, inter_dtype, row_tile):
    (w_in, gn_w, w_ret_out, w_moba_out, w_o, ln1_w, ln1_b, w_up, conv_w, conv_b, w_down, ln2_w, ln2_b) = wts
    batch, seq, _ = x.shape
    n = batch * seq
    x2d = x.reshape(n, D_MODEL)
    tm = _pick_tile(n, row_tile)
    time_minor = conv_state is None
    rq, rk, rv, rg, mq, mk, mv, gab = _in_proj(x2d, w_in, inter_dtype, tm, seq if time_minor else None)
    ro, ret_new = _retention(rq, rk, rv, rg, ret_state, gn_w, batch, seq, ret_chunk, inter_dtype, *ret_tiling)
    mo = attend(mq, mk, mv)
    if conv_state is None:
        y, conv_new = _merge_ffn_prompt(x2d, ro, mo, gab, wts, alpha, batch, seq, _pick_tile(seq, FUSED_ROW_TILE))
    else:
        h = _merge(x2d, ro, mo, gab, w_ret_out, w_moba_out, w_o, ln1_w, ln1_b, alpha, tm)
        hist = CONV_W - 1
        t_idx = jnp.arange(seq)[None, :, None]
        newest, older = conv_state[:, hist - 1][:, None, :], conv_state[:, hist - 2][:, None, :]
        prev1 = jnp.where(t_idx == 0, newest, 0.0)
        prev2 = jnp.where(t_idx == 0, older, jnp.where(t_idx == 1, newest, 0.0))
        y, ua = _ffn_sample(h, prev1.reshape(n, D_FF), prev2.reshape(n, D_FF), w_up, conv_w, conv_b, w_down,
                            ln2_w, ln2_b, alpha, seq, tm)
        conv_new = jnp.concatenate([conv_state, ua.reshape(batch, seq, D_FF)], axis=1)[:, -hist:]
    if time_minor:
        unpack = lambda a: a.reshape(batch, MOBA_HEADS, MOBA_DH, seq).transpose(0, 3, 1, 2)
    else:
        unpack = lambda a: a.reshape(batch, seq, MOBA_HEADS, MOBA_DH)
    return y.reshape(batch, seq, D_MODEL), ret_new, conv_new, unpack(mk), unpack(mv)


def kernel(x_prompt, x_sample, cache_k, cache_v, page_table, state_ret, state_conv, w_in, ret_gn_w, w_ret_out,
           w_moba_out, w_o, ln1_w, ln1_b, w_up, conv_w, conv_b, w_down, ln2_w, ln2_b):
    depth = w_in.shape[0]
    alpha = (2 * depth) ** 0.25
    batch, seq, _ = x_prompt.shape
    dec_batch, dec_seq, _ = x_sample.shape
    n_phys, page = cache_k.shape[1], cache_k.shape[2]
    slope_row = jnp.repeat(jnp.exp2(-8.0 * (jnp.arange(MOBA_HEADS, dtype=_F32) + 1.0) / MOBA_HEADS),
                           MOBA_DH)[None, :]
    hp, hs = x_prompt, x_sample
    outs = [[] for _ in range(8)]
    for l in range(depth):
        wts = (w_in[l].astype(_BF), ret_gn_w[l][None, :], w_ret_out[l].astype(_BF), w_moba_out[l].astype(_BF),
               w_o[l].astype(_BF), ln1_w[l][None, :], ln1_b[l][None, :], w_up[l].astype(_BF), conv_w[l],
               conv_b[l][None, :], w_down[l].astype(_BF), ln2_w[l][None, :], ln2_b[l][None, :])
        attend_p = functools.partial(_moba_prompt, slope_row=slope_row, batch=batch, seq=seq, out_dtype=_BF)
        hp, rp, cp, kp, vp = _layer(hp, jnp.zeros((batch, RET_HEADS, RET_DK, RET_DV), _F32), None, RET_CHUNK,
                                    (1, 1), attend_p, wts, alpha, _BF, 512)
        ck = jnp.transpose(cache_k[l], (0, 2, 3, 1)).reshape(n_phys, MOBA_W, page)
        cv = jnp.transpose(cache_v[l], (0, 2, 3, 1)).reshape(n_phys, MOBA_W, page)
        attend_s = functools.partial(_moba_sample, slope_row=slope_row, cache_kt=ck, cache_vt=cv,
                                     page_table=page_table, t_new=dec_seq, out_dtype=_F32)
        hs, rs, cs, ks_, vs_ = _layer(hs, state_ret[l], state_conv[l], dec_seq, (RET_HEADS, _pick_tile(dec_batch, 8)),
                                      attend_s, wts, alpha, _F32, 128)
        for lst, val in zip(outs, (kp, vp, rp, cp, ks_, vs_, rs, cs)):
            lst.append(val)
    return (hp, hs) + tuple(jnp.stack(o) for o in outs)
```

```python
import functools

import jax
import jax.numpy as jnp
from jax import lax
from jax.experimental import pallas as pl
from jax.experimental.pallas import tpu as pltpu

D_MODEL = 1024
RET_HEADS = 4
RET_DK = 128
RET_DV = 256
RET_CHUNK = 256
MOBA_HEADS = 8
MOBA_DH = 64
MOBA_BLOCK = 256
MOBA_TOPK = 3
D_FF = 2816
CONV_W = 3
LN_EPS = 1e-5
GN_EPS = 1e-6

RET_QK_W = RET_HEADS * RET_DK
RET_V_W = RET_HEADS * RET_DV
MOBA_W = MOBA_HEADS * MOBA_DH
IN_SPLITS = (RET_QK_W, RET_QK_W, RET_V_W, RET_V_W, MOBA_W, MOBA_W, MOBA_W, D_MODEL, D_MODEL)
N_IN = sum(IN_SPLITS)

LANES = 128
VMEM_LIMIT_BYTES = 56 * 1024 * 1024
NEG_BIG = -1e30
LOG2E = 1.4426950408889634
ONES_ROWS = 16

_BF = jnp.bfloat16
_F32 = jnp.float32


def _nt(a, b):
    return lax.dot_general(a, b, (((1,), (1,)), ((), ())), preferred_element_type=_F32)


def _nn(a, b):
    return jnp.dot(a, b, preferred_element_type=_F32)


def _split_bf16(x):
    hi = x.astype(_BF)
    lo = (x - hi.astype(_F32)).astype(_BF)
    return hi, lo


def _nt3(a, b):
    ah, al = _split_bf16(a)
    bh, bl = _split_bf16(b)
    return _nt(ah, bh) + (_nt(al, bh) + _nt(ah, bl))


def _layer_norm(x, w, b):
    mu = jnp.mean(x, axis=-1, keepdims=True)
    d = x - mu
    var = jnp.mean(d * d, axis=-1, keepdims=True)
    return d * lax.rsqrt(var + LN_EPS) * w + b


def _const_spec(shape):
    return pl.BlockSpec(shape, lambda *_: (0,) * len(shape), pipeline_mode=pl.Buffered(1))


def _params(n_grid):
    return pltpu.CompilerParams(dimension_semantics=("arbitrary",) * n_grid,
                                vmem_limit_bytes=VMEM_LIMIT_BYTES)


IN_OFFSETS = tuple(sum(IN_SPLITS[:i]) for i in range(len(IN_SPLITS) + 1))
PROJ_SEGMENTS = ((0,), (1,), (2,), (3,), (4,), (5,), (6,), (7, 8))
PROJ_PIECES = ((0,), (1,), (4,), (5, 7), (2,), (3,), (6,), ())


def _in_proj_segment(xb, w_ref, out_refs, j):
    ref = out_refs[j]
    segs = PROJ_SEGMENTS[j]
    val = _nn(xb, w_ref[:, IN_OFFSETS[segs[0]]:IN_OFFSETS[segs[-1] + 1]])
    if j == 1:
        val = val * RET_DK ** -0.5
    if len(ref.shape) == 3:
        ref[0] = val.T
    else:
        ref[...] = val.astype(ref.dtype)


def _in_proj_kernel(x_ref, w_ref, *out_refs):
    xb = x_ref[...].astype(_BF)
    for j in range(len(PROJ_SEGMENTS)):
        _in_proj_segment(xb, w_ref, out_refs, j)


def _in_proj_outputs(n, inter_dtype, tm, kv_time_minor_seq):
    widths = (RET_QK_W, RET_QK_W, RET_V_W, RET_V_W, MOBA_W, MOBA_W, MOBA_W, 2 * D_MODEL)
    dtypes = (inter_dtype,) * 4 + (_F32,) * 3 + (inter_dtype,)
    out_specs = [pl.BlockSpec((tm, w), lambda i, *_: (i, 0)) for w in widths]
    out_shape = [jax.ShapeDtypeStruct((n, w), dt) for w, dt in zip(widths, dtypes)]
    if kv_time_minor_seq is not None:
        seq = kv_time_minor_seq
        assert seq % tm == 0 and tm % LANES == 0
        tps = seq // tm
        for j in (5, 6):
            out_specs[j] = pl.BlockSpec((1, MOBA_W, tm), lambda i, *_: (i // tps, 0, i % tps))
            out_shape[j] = jax.ShapeDtypeStruct((n // seq, MOBA_W, seq), _F32)
    return out_specs, out_shape


def _in_proj(x2d, w_in_bf, inter_dtype, tm, kv_time_minor_seq=None):
    n = x2d.shape[0]
    assert n % tm == 0
    out_specs, out_shape = _in_proj_outputs(n, inter_dtype, tm, kv_time_minor_seq)
    return pl.pallas_call(
        _in_proj_kernel,
        grid=(n // tm,),
        in_specs=[pl.BlockSpec((tm, D_MODEL), lambda i: (i, 0)), _const_spec((D_MODEL, N_IN))],
        out_specs=out_specs,
        out_shape=out_shape,
        compiler_params=_params(1),
        name="in_proj",
    )(x2d, w_in_bf)


def _retention_kernel(q_ref, k_ref, v_ref, g_ref, s0_ref, intra_ref, qdec_ref, kdec_ref, cdec_ref,
                      gnw_ref, ro_ref, sn_ref, *, seq, chunk, cpad, n_heads, n_seqs):
    n_chunks = seq // chunk

    def pad(a):
        if chunk == cpad:
            return a
        return jnp.concatenate([a, jnp.zeros((cpad - chunk, a.shape[1]), a.dtype)], axis=0)

    for h in range(n_heads):
        intra = intra_ref[h]
        qdec = qdec_ref[h]
        kdec = kdec_ref[h]
        cdec = cdec_ref[h]
        kcols = slice(h * RET_DK, (h + 1) * RET_DK)
        vcols = slice(h * RET_DV, (h + 1) * RET_DV)
        gnw = gnw_ref[:, vcols]
        for bb in range(n_seqs):
            s = s0_ref[bb, h]
            for c in range(n_chunks):
                rows = slice(bb * seq + c * chunk, bb * seq + (c + 1) * chunk)
                q = pad(q_ref[rows, kcols].astype(_F32))
                k = pad(k_ref[rows, kcols].astype(_F32))
                v = pad(v_ref[rows, vcols].astype(_F32)).astype(_BF)
                a = _nt(q.astype(_BF), k.astype(_BF)) * intra
                o = _nn(a.astype(_BF), v) + _nn((q * qdec).astype(_BF), s.astype(_BF))
                s = cdec * s + _nn((k * kdec).T.astype(_BF), v)
                o = o[:chunk]
                mu = jnp.mean(o, axis=-1, keepdims=True)
                d = o - mu
                var = jnp.mean(d * d, axis=-1, keepdims=True)
                on = d * lax.rsqrt(var + GN_EPS) * gnw
                g = g_ref[rows, vcols].astype(_F32)
                ro_ref[rows, vcols] = (g * jax.nn.sigmoid(g) * on).astype(ro_ref.dtype)
            sn_ref[bb, h] = s


def _decay_tables(chunk, cpad):
    log_g = jnp.log1p(-jnp.exp2(-5.0 - jnp.arange(RET_HEADS, dtype=_F32)))
    i = jnp.arange(chunk, dtype=_F32)
    diff = i[:, None] - i[None, :]
    intra = jnp.where(diff >= 0, jnp.exp(jnp.maximum(diff, 0.0)[None] * log_g[:, None, None]), 0.0)
    q_dec = jnp.exp((i[None] + 1.0) * log_g[:, None])
    k_dec = jnp.exp((chunk - 1.0 - i[None]) * log_g[:, None])
    c_dec = jnp.exp(chunk * log_g)
    p = cpad - chunk
    intra = jnp.pad(intra, ((0, 0), (0, p), (0, p)))
    q_dec = jnp.broadcast_to(jnp.pad(q_dec, ((0, 0), (0, p)))[:, :, None], (RET_HEADS, cpad, RET_DK))
    k_dec = jnp.broadcast_to(jnp.pad(k_dec, ((0, 0), (0, p)))[:, :, None], (RET_HEADS, cpad, RET_DK))
    c_dec = jnp.broadcast_to(c_dec[:, None, None], (RET_HEADS, 1, RET_DV))
    return intra, q_dec, k_dec, c_dec


def _retention(rq, rk, rv, rg, s0, gn_w, batch, seq, chunk, out_dtype, n_heads, n_seqs):
    cpad = max(chunk, LANES)
    assert seq % chunk == 0 and batch % n_seqs == 0 and RET_HEADS % n_heads == 0
    intra, q_dec, k_dec, c_dec = _decay_tables(chunk, cpad)
    kern = functools.partial(_retention_kernel, seq=seq, chunk=chunk, cpad=cpad, n_heads=n_heads, n_seqs=n_seqs)
    head_tab = lambda shape: pl.BlockSpec((n_heads,) + shape, lambda b, h: (h, 0, 0))
    tok = lambda w: pl.BlockSpec((n_seqs * seq, n_heads * w), lambda b, h: (b, h))
    state = pl.BlockSpec((n_seqs, n_heads, RET_DK, RET_DV), lambda b, h: (b, h, 0, 0))
    return pl.pallas_call(
        kern,
        grid=(batch // n_seqs, RET_HEADS // n_heads),
        in_specs=[tok(RET_DK), tok(RET_DK), tok(RET_DV), tok(RET_DV), state,
                  head_tab((cpad, cpad)), head_tab((cpad, RET_DK)), head_tab((cpad, RET_DK)),
                  head_tab((1, RET_DV)),
                  pl.BlockSpec((1, n_heads * RET_DV), lambda b, h: (0, h))],
        out_specs=[tok(RET_DV), state],
        out_shape=[
            jax.ShapeDtypeStruct((batch * seq, RET_V_W), out_dtype),
            jax.ShapeDtypeStruct((batch, RET_HEADS, RET_DK, RET_DV), _F32),
        ],
        compiler_params=_params(2),
        name="retention",
    )(rq, rk, rv, rg, s0, intra, q_dec, k_dec, c_dec, gn_w)


def _split3_bf16(x):
    h = x.astype(_BF)
    r = x - h.astype(_F32)
    m = r.astype(_BF)
    l = (r - m.astype(_F32)).astype(_BF)
    return h, m, l


def _moba_prompt_kernel(q_ref, kt_ref, vt_ref, slope_ref, o_ref, *, seq):
    blk = MOBA_BLOCK
    nb = seq // blk
    hw = MOBA_DH
    scale = MOBA_DH ** -0.5
    q = q_ref[...] * (scale * LOG2E)
    k = kt_ref[0].T
    vt = jnp.concatenate([vt_ref[0].astype(_BF), jnp.ones((ONES_ROWS, seq), _BF)], axis=0)
    kmean = jnp.concatenate(
        [jnp.sum(k[n * blk:(n + 1) * blk], axis=0, keepdims=True) for n in range(nb)], axis=0) * (1.0 / blk)

    k_lane = lax.broadcasted_iota(jnp.int32, (seq, LANES), 1)
    k_off = (lax.broadcasted_iota(jnp.int32, (seq, LANES), 0) % blk).astype(_F32)
    k_aug = jnp.where(k_lane < 3, k_off, jnp.where(k_lane < 6, 1.0, 0.0))
    kb = jnp.concatenate([k.astype(_BF), k_aug.astype(_BF)], axis=1)

    q_lane = lax.broadcasted_iota(jnp.int32, (2 * blk, LANES), 1)
    q_row = lax.broadcasted_iota(jnp.int32, (2 * blk, LANES), 0)
    s_col = jnp.where(q_row < blk, slope_ref[0:1, 0:1], slope_ref[0:1, hw:hw + 1]) * LOG2E
    c_col = -s_col * (q_row % blk).astype(_F32)
    s3, c3 = _split3_bf16(s_col), _split3_bf16(c_col)
    q_aug = jnp.zeros((2 * blk, LANES), _F32)
    for j in range(3):
        q_aug = jnp.where(q_lane == j, s3[j].astype(_F32), jnp.where(q_lane == 3 + j, c3[j].astype(_F32), q_aug))
    q_aug = q_aug.astype(_BF)

    lane_head = lax.broadcasted_iota(jnp.int32, (blk, 2 * hw), 1) // hw
    key_i = lax.broadcasted_iota(jnp.int32, (blk, 2 * blk), 0)
    qry_i = lax.broadcasted_iota(jnp.int32, (blk, 2 * blk), 1) % blk
    causal = qry_i >= key_i
    col_head = lax.broadcasted_iota(jnp.int32, (1, 2 * blk), 1) // blk
    slope = jnp.where(col_head == 0, slope_ref[0:1, 0:1], slope_ref[0:1, hw:hw + 1]) * LOG2E
    row_id = lax.broadcasted_iota(jnp.int32, (nb, 2 * blk), 0)

    def scores(i):
        qi = q[i * blk:(i + 1) * blk]
        qexp = jnp.concatenate([jnp.where(lane_head == 0, qi, 0.0), jnp.where(lane_head == 1, qi, 0.0)], axis=0)
        st = _nt(kb[:(i + 1) * blk], jnp.concatenate([qexp.astype(_BF), q_aug], axis=1))
        return i, qexp, st

    def row_max(i, qexp, st):
        sel = None
        if i > MOBA_TOPK:
            gt = _nt3(kmean, qexp)
            valid = row_id < i
            sel = []
            for n in range(i):
                gn = gt[n:n + 1]
                ahead = ((gt > gn) | ((gt == gn) & (row_id < n))) & valid
                rank = jnp.sum(ahead.astype(_F32), axis=0, keepdims=True)
                sel.append(rank < MOBA_TOPK)
        keep = [causal if n == i else (None if sel is None else sel[n]) for n in range(i + 1)]
        shifts = [slope * float(-(i - n) * blk) for n in range(i + 1)]
        tops = []
        for n in range(i + 1):
            t = st[n * blk:(n + 1) * blk]
            if keep[n] is not None:
                t = jnp.where(keep[n], t, NEG_BIG)
            tops.append(jnp.max(t, axis=0, keepdims=True) + shifts[n])
        m = functools.reduce(jnp.maximum, tops)
        return i, st, keep, [sh - m for sh in shifts]

    def probs(i, st, keep, offs):
        ps = []
        for n in range(i + 1):
            t = st[n * blk:(n + 1) * blk] + offs[n]
            if keep[n] is not None:
                t = jnp.where(keep[n], t, NEG_BIG)
            ps.append(jnp.exp2(t).astype(_BF))
        return i, (ps[0] if i == 0 else jnp.concatenate(ps, axis=0))

    def finish(i, pt):
        ot = _nn(vt[:, :(i + 1) * blk], pt)
        ot = ot[:2 * hw] * (1.0 / ot[2 * hw:2 * hw + 1])
        oi = jnp.concatenate([ot[0:hw, 0:blk], ot[hw:2 * hw, blk:2 * blk]], axis=0)
        o_ref[i * blk:(i + 1) * blk, :] = oi.T.astype(o_ref.dtype)

    stages = (scores, row_max, probs, finish)
    in_flight = [None] * len(stages)
    for step in range(nb + len(stages) - 1):
        for j in reversed(range(1, len(stages))):
            if in_flight[j] is not None:
                out = stages[j](*in_flight[j])
                in_flight[j] = None
                if j + 1 < len(stages):
                    in_flight[j + 1] = out
        if step < nb:
            in_flight[1] = scores(step)


def _moba_prompt(mq, mkt, mvt, slope_row, batch, seq, out_dtype):
    width = 2 * MOBA_DH
    assert width == LANES and seq % MOBA_BLOCK == 0
    kern = functools.partial(_moba_prompt_kernel, seq=seq)
    spec = pl.BlockSpec((seq, width), lambda b, h: (b, h))
    spec_t = pl.BlockSpec((1, width, seq), lambda b, h: (b, h, 0))
    return pl.pallas_call(
        kern,
        grid=(batch, MOBA_W // width),
        in_specs=[spec, spec_t, spec_t, pl.BlockSpec((1, width), lambda b, h: (0, h))],
        out_specs=spec,
        out_shape=jax.ShapeDtypeStruct((batch * seq, MOBA_W), out_dtype),
        compiler_params=_params(2),
        name="moba_prompt",
    )(mq, mkt, mvt, slope_row)


RING_SLOTS = 4
KEY_GROUPS = 4


def _moba_sample_kernel(pt_ref, q_ref, kn_ref, vn_ref, slope_ref, ck_hbm, cv_hbm, *rest,
                        n_pages, page, t_new, with_proj):
    if with_proj:
        x_ref, w_ref, o_ref = rest[:3]
        proj_refs = rest[3:11]
        ring, s_scr, p_scr, bias_scr, sem = rest[11:]
        xb = x_ref[...].astype(_BF)
    else:
        o_ref, ring, s_scr, p_scr, bias_scr, sem = rest

    def proj_piece(c):
        if with_proj:
            for j in PROJ_PIECES[c]:
                _in_proj_segment(xb, w_ref, proj_refs, j)

    b = pl.program_id(0)
    n_seq = pl.num_programs(0)
    blk = MOBA_BLOCK
    past = n_pages * page
    nb = past // blk
    rows = MOBA_HEADS * t_new
    scale = MOBA_DH ** -0.5
    ppc = n_pages // KEY_GROUPS
    cw = ppc * page
    bpc = cw // blk
    n_chunks = 2 * KEY_GROUPS
    lead = RING_SLOTS - 1

    def copies(seq_idx, c):
        src = ck_hbm if c < KEY_GROUPS else cv_hbm
        first = (c % KEY_GROUPS) * ppc
        slot = c % RING_SLOTS
        return [pltpu.make_async_copy(src.at[pt_ref[seq_idx, first + j]],
                                      ring.at[slot, :, pl.ds(j * page, page)], sem.at[slot])
                for j in range(ppc)]

    def start(seq_idx, c):
        for d in copies(seq_idx, c):
            d.start()

    q_rows = jnp.concatenate([q_ref[...]] * MOBA_HEADS, axis=0)
    r_head = lax.broadcasted_iota(jnp.int32, (rows, MOBA_W), 0) // t_new
    c_head = lax.broadcasted_iota(jnp.int32, (rows, MOBA_W), 1) // MOBA_DH
    own = r_head == c_head
    q_hi, q_lo = _split_bf16(jnp.where(own, q_rows, 0.0) * scale)
    q_stack = jnp.concatenate([q_hi, q_lo], axis=0)
    slope = jnp.sum(jnp.where(own, jnp.broadcast_to(slope_ref[...], (rows, MOBA_W)), 0.0),
                    axis=-1, keepdims=True) * (1.0 / MOBA_DH)
    tok = lax.broadcasted_iota(jnp.int32, (rows, blk), 0) % t_new
    off = lax.broadcasted_iota(jnp.int32, (rows, blk), 1)

    @pl.when(b == 0)
    def _():
        for c in range(lead):
            start(0, c)
        for n in range(nb):
            dist = (past - n * blk) + (tok - off)
            bias_scr[:, n * blk:(n + 1) * blk] = -slope * dist.astype(_F32)

    gate = jnp.zeros((rows, nb), _F32)
    bid = lax.broadcasted_iota(jnp.int32, (rows, nb), 1)
    acc = den = None
    for c in range(n_chunks):
        nxt = c + lead
        if nxt < n_chunks:
            start(b, nxt)
        else:
            @pl.when(b + 1 < n_seq)
            def _():
                start(b + 1, nxt - n_chunks)
        for d in copies(b, c):
            d.wait()
        slot = c % RING_SLOTS
        if c < KEY_GROUPS:
            kc = ring[slot]
            k_hi, k_lo = _split_bf16(kc)
            r1 = _nn(q_stack, k_hi)
            s_scr[:, c * cw:(c + 1) * cw] = r1[:rows]
            g = r1[:rows] + (r1[rows:] + _nn(q_hi, k_lo))
            for j in range(bpc):
                gs = jnp.sum(g[:, j * blk:(j + 1) * blk], axis=-1, keepdims=True)
                gate = jnp.where(bid == c * bpc + j, gs, gate)
        else:
            cc = c - KEY_GROUPS
            acc = acc + _nt(p_scr[:, cc * cw:(cc + 1) * cw], ring[slot].astype(_BF))
        proj_piece(c)
        if c == KEY_GROUPS - 1:
            sel = jnp.zeros((rows, nb), jnp.bool_)
            for _ in range(min(MOBA_TOPK, nb)):
                mx = jnp.max(gate, axis=-1, keepdims=True)
                first = jnp.min(jnp.where(gate == mx, bid, nb), axis=-1, keepdims=True)
                hit = bid == first
                sel = sel | hit
                gate = jnp.where(hit, -jnp.inf, gate)
            zpad = jnp.zeros((page - t_new, MOBA_W), _F32)
            kn = jnp.concatenate([kn_ref[...], zpad], axis=0).astype(_BF)
            vn = jnp.concatenate([vn_ref[...], zpad], axis=0).astype(_BF)
            tok_p = lax.broadcasted_iota(jnp.int32, (rows, page), 0) % t_new
            off_p = lax.broadcasted_iota(jnp.int32, (rows, page), 1)
            s_own = _nt(q_hi, kn) - slope * (tok_p - off_p).astype(_F32)
            s_own = jnp.where(off_p <= tok_p, s_own, NEG_BIG)
            mrun = jnp.full((rows, blk), NEG_BIG, _F32)
            for n in range(nb):
                cols = slice(n * blk, (n + 1) * blk)
                t = jnp.where(sel[:, n:n + 1], s_scr[:, cols] + bias_scr[:, cols], NEG_BIG)
                s_scr[:, cols] = t
                mrun = jnp.maximum(mrun, t)
            m = jnp.maximum(jnp.max(mrun, axis=-1, keepdims=True), jnp.max(s_own, axis=-1, keepdims=True))
            p_own = jnp.exp(s_own - m)
            lrun = jnp.zeros((rows, blk), _F32)
            for n in range(nb):
                cols = slice(n * blk, (n + 1) * blk)
                e = jnp.exp(s_scr[:, cols] - m)
                lrun = lrun + e
                p_scr[:, cols] = e.astype(_BF)
            den = jnp.sum(lrun, axis=-1, keepdims=True) + jnp.sum(p_own, axis=-1, keepdims=True)
            acc = _nn(p_own.astype(_BF), vn)

    out = acc * (1.0 / den)
    o_ref[...] = jnp.concatenate(
        [out[h * t_new:(h + 1) * t_new, h * MOBA_DH:(h + 1) * MOBA_DH] for h in range(MOBA_HEADS)],
        axis=1).astype(o_ref.dtype)


def _moba_sample(mq, mk, mv, slope_row, cache_kt, cache_vt, page_table, t_new, out_dtype, proj=None):
    batch, n_pages = page_table.shape
    page = cache_kt.shape[2]
    assert page == LANES and t_new <= page and MOBA_BLOCK % page == 0
    assert n_pages % KEY_GROUPS == 0 and (n_pages // KEY_GROUPS * page) % MOBA_BLOCK == 0
    assert (2 * KEY_GROUPS) % RING_SLOTS == 0
    past = n_pages * page
    cw = n_pages // KEY_GROUPS * page
    rows = MOBA_HEADS * t_new
    kern = functools.partial(_moba_sample_kernel, n_pages=n_pages, page=page, t_new=t_new,
                             with_proj=proj is not None)
    tok = pl.BlockSpec((t_new, MOBA_W), lambda b, pt: (b, 0))
    in_specs = [tok, tok, tok, pl.BlockSpec((1, MOBA_W), lambda b, pt: (0, 0)),
                pl.BlockSpec(memory_space=pl.ANY), pl.BlockSpec(memory_space=pl.ANY)]
    out_specs, out_shape = tok, jax.ShapeDtypeStruct((batch * t_new, MOBA_W), out_dtype)
    operands = (page_table, mq, mk, mv, slope_row, cache_kt, cache_vt)
    if proj is not None:
        x2d, w_in_bf, inter_dtype, seq = proj
        tmx = x2d.shape[0] // batch
        assert x2d.shape[0] == tmx * batch
        p_specs, p_shape = _in_proj_outputs(x2d.shape[0], inter_dtype, tmx, seq)
        in_specs += [pl.BlockSpec((tmx, D_MODEL), lambda b, pt: (b, 0)), _const_spec((D_MODEL, N_IN))]
        out_specs, out_shape = [out_specs] + p_specs, [out_shape] + p_shape
        operands += (x2d, w_in_bf)
    grid_spec = pltpu.PrefetchScalarGridSpec(
        num_scalar_prefetch=1,
        grid=(batch,),
        in_specs=in_specs,
        out_specs=out_specs,
        scratch_shapes=[
            pltpu.VMEM((RING_SLOTS, MOBA_W, cw), _F32),
            pltpu.VMEM((rows, past), _F32),
            pltpu.VMEM((rows, past), _BF),
            pltpu.VMEM((rows, past), _F32),
            pltpu.SemaphoreType.DMA((RING_SLOTS,)),
        ],
    )
    res = pl.pallas_call(
        kern,
        grid_spec=grid_spec,
        out_shape=out_shape,
        compiler_params=_params(1),
        name="moba_sample",
    )(*operands)
    return res if proj is None else (res[0], tuple(res[1:]))


def _merge_kernel(x_ref, ro_ref, mo_ref, gab_ref, wr_ref, wm_ref, wo_ref, lnw_ref, lnb_ref, h_ref, *, alpha):
    a = _nn(ro_ref[...].astype(_BF), wr_ref[...])
    m = _nn(mo_ref[...].astype(_BF), wm_ref[...])
    ga = gab_ref[:, :D_MODEL].astype(_F32)
    gb = gab_ref[:, D_MODEL:].astype(_F32)
    merged = jax.nn.sigmoid(ga) * a + jax.nn.sigmoid(gb) * m
    pre = alpha * x_ref[...] + _nn(merged.astype(_BF), wo_ref[...])
    h_ref[...] = _layer_norm(pre, lnw_ref[...], lnb_ref[...])


def _merge(x2d, ro, mo, gab, w_ret_out, w_moba_out, w_o, ln_w, ln_b, alpha, tm):
    n = x2d.shape[0]
    assert n % tm == 0
    row = lambda w: pl.BlockSpec((tm, w), lambda i: (i, 0))
    return pl.pallas_call(
        functools.partial(_merge_kernel, alpha=alpha),
        grid=(n // tm,),
        in_specs=[row(D_MODEL), row(RET_V_W), row(MOBA_W), row(2 * D_MODEL),
                  _const_spec((RET_V_W, D_MODEL)), _const_spec((MOBA_W, D_MODEL)),
                  _const_spec((D_MODEL, D_MODEL)), _const_spec((1, D_MODEL)), _const_spec((1, D_MODEL))],
        out_specs=row(D_MODEL),
        out_shape=jax.ShapeDtypeStruct((n, D_MODEL), _F32),
        compiler_params=_params(1),
        name="merge",
    )(x2d, ro, mo, gab, w_ret_out, w_moba_out, w_o, ln_w, ln_b)


FFN_COL_CHUNK = D_FF
CONV_PAD = 8
FUSED_ROW_TILE = 512
LONG_ROW_TILE = 512
SHORT_ROW_TILE = 128
PROJ_RIDE_TILES = (128, 256, 512)
FUSED_SUBTILES = 2


def _ffn_kernel(*refs, alpha, tm, seq, within_seq):
    if within_seq:
        (h_ref, wup_ref, cw_ref, cb_ref, wdn_ref, lnw_ref, lnb_ref, y_ref, cs_ref, ubuf) = refs
    else:
        (h_ref, p1_ref, p2_ref, wup_ref, cw_ref, cb_ref, wdn_ref, lnw_ref, lnb_ref, y_ref, ua_ref, ubuf) = refs
    h = h_ref[...]
    hb = h.astype(_BF)
    hist = CONV_W - 1

    if within_seq:
        tiles_per_seq = seq // tm

        @pl.when(pl.program_id(0) % tiles_per_seq == 0)
        def _():
            ubuf[0:CONV_PAD, :] = jnp.zeros((CONV_PAD, D_FF), _F32)
    else:
        t_in_seq = lax.broadcasted_iota(jnp.int32, (tm, 1), 0) % seq

        @pl.when(pl.program_id(0) == 0)
        def _():
            ubuf[0:CONV_PAD, :] = jnp.zeros((CONV_PAD, D_FF), _F32)

    f = jnp.zeros((tm, D_MODEL), _F32)
    for c0 in range(0, D_FF, FFN_COL_CHUNK):
        cols = slice(c0, c0 + FFN_COL_CHUNK)
        ua = _nn(hb, wup_ref[:, c0:c0 + FFN_COL_CHUNK])
        ug = _nn(hb, wup_ref[:, D_FF + c0:D_FF + c0 + FFN_COL_CHUNK])
        ubuf[CONV_PAD:CONV_PAD + tm, cols] = ua
        sh1 = ubuf[CONV_PAD - 1:CONV_PAD - 1 + tm, cols]
        sh2 = ubuf[CONV_PAD - 2:CONV_PAD - 2 + tm, cols]
        if not within_seq:
            sh1 = jnp.where(t_in_seq >= 1, sh1, p1_ref[:, cols])
            sh2 = jnp.where(t_in_seq >= 2, sh2, p2_ref[:, cols])
            ua_ref[:, cols] = ua
        uc = cb_ref[:, cols] + (sh2 * cw_ref[0:1, cols] + sh1 * cw_ref[1:2, cols] + ua * cw_ref[2:3, cols])
        act = (uc * jax.nn.sigmoid(uc) * ug).astype(_BF)
        f = f + _nn(act, wdn_ref[c0:c0 + FFN_COL_CHUNK, :])

    if within_seq:
        last = ubuf[CONV_PAD + tm - hist:CONV_PAD + tm, :]
        cs_ref[0] = last
        ubuf[CONV_PAD - hist:CONV_PAD, :] = last
    y_ref[...] = _layer_norm(alpha * h + f, lnw_ref[...], lnb_ref[...])


def _merge_ffn_kernel(x_ref, ro_ref, mo_ref, gab_ref, wr_ref, wm_ref, wo_ref, l1w_ref, l1b_ref,
                      wup_ref, cw_ref, cb_ref, wdn_ref, l2w_ref, l2b_ref, y_ref, cs_ref, ubuf, *, alpha, tm, seq):
    hist = CONV_W - 1
    sub = tm // FUSED_SUBTILES

    @pl.when(pl.program_id(0) % (seq // tm) == 0)
    def _():
        ubuf[0:CONV_PAD, :] = jnp.zeros((CONV_PAD, D_FF), _F32)

    def mix_mm(r):
        rows = slice(r * sub, (r + 1) * sub)
        return r, _nn(ro_ref[rows, :].astype(_BF), wr_ref[...]), _nn(mo_ref[rows, :].astype(_BF), wm_ref[...])

    def gate(r, a, m):
        rows = slice(r * sub, (r + 1) * sub)
        ga = gab_ref[rows, :D_MODEL].astype(_F32)
        gb = gab_ref[rows, D_MODEL:].astype(_F32)
        return r, (jax.nn.sigmoid(ga) * a + jax.nn.sigmoid(gb) * m).astype(_BF)

    def out_mm(r, merged):
        return r, _nn(merged, wo_ref[...])

    def norm1(r, proj):
        rows = slice(r * sub, (r + 1) * sub)
        h = _layer_norm(alpha * x_ref[rows, :] + proj, l1w_ref[...], l1b_ref[...])
        return r, h, h.astype(_BF)

    def up_mm(r, h, hb):
        ua = _nn(hb, wup_ref[:, :D_FF])
        ug = _nn(hb, wup_ref[:, D_FF:])
        ubuf[CONV_PAD + r * sub:CONV_PAD + (r + 1) * sub, :] = ua
        return r, h, ua, ug

    def conv_act(r, h, ua, ug):
        lo = CONV_PAD + r * sub
        sh1 = ubuf[lo - 1:lo - 1 + sub, :]
        sh2 = ubuf[lo - 2:lo - 2 + sub, :]
        uc = cb_ref[...] + (sh2 * cw_ref[0:1, :] + sh1 * cw_ref[1:2, :] + ua * cw_ref[2:3, :])
        return r, h, (uc * jax.nn.sigmoid(uc) * ug).astype(_BF)

    def down_mm(r, h, act):
        return r, h, _nn(act, wdn_ref[...])

    def norm2(r, h, f):
        y_ref[r * sub:(r + 1) * sub, :] = _layer_norm(alpha * h + f, l2w_ref[...], l2b_ref[...])

    stages = (mix_mm, gate, out_mm, norm1, up_mm, conv_act, down_mm, norm2)
    in_flight = [None] * len(stages)
    for step in range(FUSED_SUBTILES + len(stages) - 1):
        for j in reversed(range(1, len(stages))):
            if in_flight[j] is not None:
                out = stages[j](*in_flight[j])
                in_flight[j] = None
                if j + 1 < len(stages):
                    in_flight[j + 1] = out
        if step < FUSED_SUBTILES:
            in_flight[1] = mix_mm(step)

    last = ubuf[CONV_PAD + tm - hist:CONV_PAD + tm, :]
    cs_ref[0] = last
    ubuf[CONV_PAD - hist:CONV_PAD, :] = last


def _merge_ffn_prompt(x2d, ro, mo, gab, wts, alpha, batch, seq, tm):
    (_, _, w_ret_out, w_moba_out, w_o, ln1_w, ln1_b, w_up, conv_w, conv_b, w_down, ln2_w, ln2_b) = wts
    n = x2d.shape[0]
    assert seq % tm == 0 and tm >= CONV_W - 1
    tiles_per_seq = seq // tm
    row = lambda w: pl.BlockSpec((tm, w), lambda i: (i, 0))
    return pl.pallas_call(
        functools.partial(_merge_ffn_kernel, alpha=alpha, tm=tm, seq=seq),
        grid=(n // tm,),
        in_specs=[row(D_MODEL), row(RET_V_W), row(MOBA_W), row(2 * D_MODEL),
                  _const_spec((RET_V_W, D_MODEL)), _const_spec((MOBA_W, D_MODEL)), _const_spec((D_MODEL, D_MODEL)),
                  _const_spec((1, D_MODEL)), _const_spec((1, D_MODEL)),
                  _const_spec((D_MODEL, 2 * D_FF)), _const_spec((CONV_W, D_FF)), _const_spec((1, D_FF)),
                  _const_spec((D_FF, D_MODEL)), _const_spec((1, D_MODEL)), _const_spec((1, D_MODEL))],
        out_specs=[row(D_MODEL), pl.BlockSpec((1, CONV_W - 1, D_FF), lambda i: (i // tiles_per_seq, 0, 0))],
        out_shape=[jax.ShapeDtypeStruct((n, D_MODEL), _F32),
                   jax.ShapeDtypeStruct((batch, CONV_W - 1, D_FF), _F32)],
        scratch_shapes=[pltpu.VMEM((CONV_PAD + tm, D_FF), _F32)],
        compiler_params=_params(1),
        name="merge_ffn",
    )(x2d, ro, mo, gab, w_ret_out, w_moba_out, w_o, ln1_w, ln1_b, w_up, conv_w, conv_b, w_down, ln2_w, ln2_b)


def _ffn_prompt(h2d, w_up, conv_w, conv_b, w_down, ln_w, ln_b, alpha, batch, seq, tm):
    n = h2d.shape[0]
    assert seq % tm == 0 and tm >= CONV_W - 1
    tiles_per_seq = seq // tm
    row = pl.BlockSpec((tm, D_MODEL), lambda i: (i, 0))
    return pl.pallas_call(
        functools.partial(_ffn_kernel, alpha=alpha, tm=tm, seq=seq, within_seq=True),
        grid=(n // tm,),
        in_specs=[row, _const_spec((D_MODEL, 2 * D_FF)), _const_spec((CONV_W, D_FF)), _const_spec((1, D_FF)),
                  _const_spec((D_FF, D_MODEL)), _const_spec((1, D_MODEL)), _const_spec((1, D_MODEL))],
        out_specs=[row, pl.BlockSpec((1, CONV_W - 1, D_FF), lambda i: (i // tiles_per_seq, 0, 0))],
        out_shape=[jax.ShapeDtypeStruct((n, D_MODEL), _F32),
                   jax.ShapeDtypeStruct((batch, CONV_W - 1, D_FF), _F32)],
        scratch_shapes=[pltpu.VMEM((CONV_PAD + tm, D_FF), _F32)],
        compiler_params=_params(1),
        name="ffn_prompt",
    )(h2d, w_up, conv_w, conv_b, w_down, ln_w, ln_b)


def _ffn_sample(h2d, prev1, prev2, w_up, conv_w, conv_b, w_down, ln_w, ln_b, alpha, seq, tm):
    n = h2d.shape[0]
    assert n % tm == 0 and tm % seq == 0
    row = pl.BlockSpec((tm, D_MODEL), lambda i: (i, 0))
    wide = pl.BlockSpec((tm, D_FF), lambda i: (i, 0))
    return pl.pallas_call(
        functools.partial(_ffn_kernel, alpha=alpha, tm=tm, seq=seq, within_seq=False),
        grid=(n // tm,),
        in_specs=[row, wide, wide, _const_spec((D_MODEL, 2 * D_FF)), _const_spec((CONV_W, D_FF)),
                  _const_spec((1, D_FF)), _const_spec((D_FF, D_MODEL)), _const_spec((1, D_MODEL)),
                  _const_spec((1, D_MODEL))],
        out_specs=[row, wide],
        out_shape=[jax.ShapeDtypeStruct((n, D_MODEL), _F32), jax.ShapeDtypeStruct((n, D_FF), _F32)],
        scratch_shapes=[pltpu.VMEM((CONV_PAD + tm, D_FF), _F32)],
        compiler_params=_params(1),
        name="ffn_sample",
    )(h2d, prev1, prev2, w_up, conv_w, conv_b, w_down, ln_w, ln_b)


def _pick_tile(n, cap):
    t = cap
    while n % t:
        t //= 2
    return t


def _layer(x, proj, ret_state, conv_state, ret_chunk, ret_tiling, attend, wts, alpha, inter_dtype, row_tile):
    (w_in, gn_w, w_ret_out, w_moba_out, w_o, ln1_w, ln1_b, w_up, conv_w, conv_b, w_down, ln2_w, ln2_b) = wts
    batch, seq, _ = x.shape
    n = batch * seq
    x2d = x.reshape(n, D_MODEL)
    tm = _pick_tile(n, row_tile)
    time_minor = conv_state is None
    rq, rk, rv, rg, mq, mk, mv, gab = proj
    ro, ret_new = _retention(rq, rk, rv, rg, ret_state, gn_w, batch, seq, ret_chunk, inter_dtype, *ret_tiling)
    mo = attend(mq, mk, mv)
    if conv_state is None:
        y, conv_new = _merge_ffn_prompt(x2d, ro, mo, gab, wts, alpha, batch, seq, _pick_tile(seq, FUSED_ROW_TILE))
    else:
        h = _merge(x2d, ro, mo, gab, w_ret_out, w_moba_out, w_o, ln1_w, ln1_b, alpha, tm)
        hist = CONV_W - 1
        t_idx = jnp.arange(seq)[None, :, None]
        newest, older = conv_state[:, hist - 1][:, None, :], conv_state[:, hist - 2][:, None, :]
        prev1 = jnp.where(t_idx == 0, newest, 0.0)
        prev2 = jnp.where(t_idx == 0, older, jnp.where(t_idx == 1, newest, 0.0))
        y, ua = _ffn_sample(h, prev1.reshape(n, D_FF), prev2.reshape(n, D_FF), w_up, conv_w, conv_b, w_down,
                            ln2_w, ln2_b, alpha, seq, tm)
        conv_new = jnp.concatenate([conv_state, ua.reshape(batch, seq, D_FF)], axis=1)[:, -hist:]
    if time_minor:
        unpack = lambda a: a.reshape(batch, MOBA_HEADS, MOBA_DH, seq).transpose(0, 3, 1, 2)
    else:
        unpack = lambda a: a.reshape(batch, seq, MOBA_HEADS, MOBA_DH)
    return y.reshape(batch, seq, D_MODEL), ret_new, conv_new, unpack(mk), unpack(mv)


def kernel(x_prompt, x_sample, cache_k, cache_v, page_table, state_ret, state_conv, w_in, ret_gn_w, w_ret_out,
           w_moba_out, w_o, ln1_w, ln1_b, w_up, conv_w, conv_b, w_down, ln2_w, ln2_b):
    depth = w_in.shape[0]
    alpha = (2 * depth) ** 0.25
    batch, seq, _ = x_prompt.shape
    dec_batch, dec_seq, _ = x_sample.shape
    n_phys, page = cache_k.shape[1], cache_k.shape[2]
    slope_row = jnp.repeat(jnp.exp2(-8.0 * (jnp.arange(MOBA_HEADS, dtype=_F32) + 1.0) / MOBA_HEADS),
                           MOBA_DH)[None, :]
    hp, hs = x_prompt, x_sample
    outs = [[] for _ in range(8)]
    for l in range(depth):
        wts = (w_in[l].astype(_BF), ret_gn_w[l][None, :], w_ret_out[l].astype(_BF), w_moba_out[l].astype(_BF),
               w_o[l].astype(_BF), ln1_w[l][None, :], ln1_b[l][None, :], w_up[l].astype(_BF), conv_w[l],
               conv_b[l][None, :], w_down[l].astype(_BF), ln2_w[l][None, :], ln2_b[l][None, :])
        xp2d = hp.reshape(batch * seq, D_MODEL)
        xs2d = hs.reshape(dec_batch * dec_seq, D_MODEL)
        ck = jnp.transpose(cache_k[l], (0, 2, 3, 1)).reshape(n_phys, MOBA_W, page)
        cv = jnp.transpose(cache_v[l], (0, 2, 3, 1)).reshape(n_phys, MOBA_W, page)
        proj_s = _in_proj(xs2d, wts[0], _F32, _pick_tile(dec_batch * dec_seq, SHORT_ROW_TILE))
        moba_s = functools.partial(_moba_sample, proj_s[4], proj_s[5], proj_s[6], slope_row, ck, cv, page_table,
                                   dec_seq, _F32)
        tile_p = (batch * seq) // dec_batch
        if tile_p * dec_batch == batch * seq and tile_p in PROJ_RIDE_TILES and seq % tile_p == 0:
            mo_s, proj_p = moba_s(proj=(xp2d, wts[0], _BF, seq))
        else:
            mo_s = moba_s()
            proj_p = _in_proj(xp2d, wts[0], _BF, _pick_tile(batch * seq, LONG_ROW_TILE), seq)
        attend_p = functools.partial(_moba_prompt, slope_row=slope_row, batch=batch, seq=seq, out_dtype=_BF)
        hp, rp, cp, kp, vp = _layer(hp, proj_p, jnp.zeros((batch, RET_HEADS, RET_DK, RET_DV), _F32), None,
                                    RET_CHUNK, (1, 1), attend_p, wts, alpha, _BF, LONG_ROW_TILE)
        hs, rs, cs, ks_, vs_ = _layer(hs, proj_s, state_ret[l], state_conv[l], dec_seq,
                                      (RET_HEADS, _pick_tile(dec_batch, 8)), lambda *_: mo_s, wts, alpha, _F32,
                                      SHORT_ROW_TILE)
        for lst, val in zip(outs, (kp, vp, rp, cp, ks_, vs_, rs, cs)):
            lst.append(val)
    return (hp, hs) + tuple(jnp.stack(o) for o in outs)
```

```python
import functools

import jax
import jax.numpy as jnp
from jax import lax
from jax.experimental import pallas as pl
from jax.experimental.pallas import tpu as pltpu

D_MODEL = 1024
RET_HEADS = 4
RET_DK = 128
RET_DV = 256
RET_CHUNK = 256
MOBA_HEADS = 8
MOBA_DH = 64
MOBA_BLOCK = 256
MOBA_TOPK = 3
D_FF = 2816
CONV_W = 3
LN_EPS = 1e-5
GN_EPS = 1e-6

RET_QK_W = RET_HEADS * RET_DK
RET_V_W = RET_HEADS * RET_DV
MOBA_W = MOBA_HEADS * MOBA_DH
IN_SPLITS = (RET_QK_W, RET_QK_W, RET_V_W, RET_V_W, MOBA_W, MOBA_W, MOBA_W, D_MODEL, D_MODEL)
N_IN = sum(IN_SPLITS)

LANES = 128
VMEM_LIMIT_BYTES = 56 * 1024 * 1024
NEG_BIG = -1e30
LOG2E = 1.4426950408889634
ONES_ROWS = 16

_BF = jnp.bfloat16
_F32 = jnp.float32


def _nt(a, b):
    return lax.dot_general(a, b, (((1,), (1,)), ((), ())), preferred_element_type=_F32)


def _nn(a, b):
    return jnp.dot(a, b, preferred_element_type=_F32)


def _split_bf16(x):
    hi = x.astype(_BF)
    lo = (x - hi.astype(_F32)).astype(_BF)
    return hi, lo


def _nt3(a, b):
    ah, al = _split_bf16(a)
    bh, bl = _split_bf16(b)
    return _nt(ah, bh) + (_nt(al, bh) + _nt(ah, bl))


def _layer_norm(x, w, b):
    mu = jnp.mean(x, axis=-1, keepdims=True)
    d = x - mu
    var = jnp.mean(d * d, axis=-1, keepdims=True)
    return d * lax.rsqrt(var + LN_EPS) * w + b


def _const_spec(shape):
    return pl.BlockSpec(shape, lambda *_: (0,) * len(shape), pipeline_mode=pl.Buffered(1))


def _params(n_grid):
    return pltpu.CompilerParams(dimension_semantics=("arbitrary",) * n_grid,
                                vmem_limit_bytes=VMEM_LIMIT_BYTES)


IN_OFFSETS = tuple(sum(IN_SPLITS[:i]) for i in range(len(IN_SPLITS) + 1))
PROJ_SEGMENTS = ((0,), (1,), (2,), (3,), (4,), (5,), (6,), (7, 8))
PROJ_PIECES = ((0,), (1,), (4,), (5,), (7,), (2,), (3,), (6,))


def _in_proj_segment(xb, w_ref, out_refs, j):
    ref = out_refs[j]
    segs = PROJ_SEGMENTS[j]
    val = _nn(xb, w_ref[:, IN_OFFSETS[segs[0]]:IN_OFFSETS[segs[-1] + 1]])
    if j == 1:
        val = val * RET_DK ** -0.5
    if len(ref.shape) == 3:
        ref[0] = val.T
    else:
        ref[...] = val.astype(ref.dtype)


def _in_proj_kernel(x_ref, w_ref, *out_refs):
    xb = x_ref[...].astype(_BF)
    for j in range(len(PROJ_SEGMENTS)):
        _in_proj_segment(xb, w_ref, out_refs, j)


def _in_proj_outputs(n, inter_dtype, tm, kv_time_minor_seq):
    widths = (RET_QK_W, RET_QK_W, RET_V_W, RET_V_W, MOBA_W, MOBA_W, MOBA_W, 2 * D_MODEL)
    dtypes = (inter_dtype,) * 4 + (_F32,) * 3 + (inter_dtype,)
    out_specs = [pl.BlockSpec((tm, w), lambda i, *_: (i, 0)) for w in widths]
    out_shape = [jax.ShapeDtypeStruct((n, w), dt) for w, dt in zip(widths, dtypes)]
    if kv_time_minor_seq is not None:
        seq = kv_time_minor_seq
        assert seq % tm == 0 and tm % LANES == 0
        tps = seq // tm
        for j in (5, 6):
            out_specs[j] = pl.BlockSpec((1, MOBA_W, tm), lambda i, *_: (i // tps, 0, i % tps))
            out_shape[j] = jax.ShapeDtypeStruct((n // seq, MOBA_W, seq), _F32)
    return out_specs, out_shape


def _in_proj(x2d, w_in_bf, inter_dtype, tm, kv_time_minor_seq=None):
    n = x2d.shape[0]
    assert n % tm == 0
    out_specs, out_shape = _in_proj_outputs(n, inter_dtype, tm, kv_time_minor_seq)
    return pl.pallas_call(
        _in_proj_kernel,
        grid=(n // tm,),
        in_specs=[pl.BlockSpec((tm, D_MODEL), lambda i: (i, 0)), _const_spec((D_MODEL, N_IN))],
        out_specs=out_specs,
        out_shape=out_shape,
        compiler_params=_params(1),
        name="in_proj",
    )(x2d, w_in_bf)


def _retention_kernel(q_ref, k_ref, v_ref, g_ref, s0_ref, intra_ref, qdec_ref, kdec_ref, cdec_ref,
                      gnw_ref, ro_ref, sn_ref, *, seq, chunk, cpad, n_heads, n_seqs):
    n_chunks = seq // chunk

    def pad(a):
        if chunk == cpad:
            return a
        return jnp.concatenate([a, jnp.zeros((cpad - chunk, a.shape[1]), a.dtype)], axis=0)

    for h in range(n_heads):
        intra = intra_ref[h]
        qdec = qdec_ref[h]
        kdec = kdec_ref[h]
        cdec = cdec_ref[h]
        kcols = slice(h * RET_DK, (h + 1) * RET_DK)
        vcols = slice(h * RET_DV, (h + 1) * RET_DV)
        gnw = gnw_ref[:, vcols]
        for bb in range(n_seqs):
            s = s0_ref[bb, h]
            for c in range(n_chunks):
                rows = slice(bb * seq + c * chunk, bb * seq + (c + 1) * chunk)
                q = pad(q_ref[rows, kcols].astype(_F32))
                k = pad(k_ref[rows, kcols].astype(_F32))
                v = pad(v_ref[rows, vcols].astype(_F32)).astype(_BF)
                a = _nt(q.astype(_BF), k.astype(_BF)) * intra
                o = _nn(a.astype(_BF), v) + _nn((q * qdec).astype(_BF), s.astype(_BF))
                s = cdec * s + _nn((k * kdec).T.astype(_BF), v)
                o = o[:chunk]
                mu = jnp.mean(o, axis=-1, keepdims=True)
                d = o - mu
                var = jnp.mean(d * d, axis=-1, keepdims=True)
                on = d * lax.rsqrt(var + GN_EPS) * gnw
                g = g_ref[rows, vcols].astype(_F32)
                ro_ref[rows, vcols] = (g * jax.nn.sigmoid(g) * on).astype(ro_ref.dtype)
            sn_ref[bb, h] = s


def _decay_tables(chunk, cpad):
    log_g = jnp.log1p(-jnp.exp2(-5.0 - jnp.arange(RET_HEADS, dtype=_F32)))
    i = jnp.arange(chunk, dtype=_F32)
    diff = i[:, None] - i[None, :]
    intra = jnp.where(diff >= 0, jnp.exp(jnp.maximum(diff, 0.0)[None] * log_g[:, None, None]), 0.0)
    q_dec = jnp.exp((i[None] + 1.0) * log_g[:, None])
    k_dec = jnp.exp((chunk - 1.0 - i[None]) * log_g[:, None])
    c_dec = jnp.exp(chunk * log_g)
    p = cpad - chunk
    intra = jnp.pad(intra, ((0, 0), (0, p), (0, p)))
    q_dec = jnp.broadcast_to(jnp.pad(q_dec, ((0, 0), (0, p)))[:, :, None], (RET_HEADS, cpad, RET_DK))
    k_dec = jnp.broadcast_to(jnp.pad(k_dec, ((0, 0), (0, p)))[:, :, None], (RET_HEADS, cpad, RET_DK))
    c_dec = jnp.broadcast_to(c_dec[:, None, None], (RET_HEADS, 1, RET_DV))
    return intra, q_dec, k_dec, c_dec


def _retention(rq, rk, rv, rg, s0, gn_w, batch, seq, chunk, out_dtype, n_heads, n_seqs):
    cpad = max(chunk, LANES)
    assert seq % chunk == 0 and batch % n_seqs == 0 and RET_HEADS % n_heads == 0
    intra, q_dec, k_dec, c_dec = _decay_tables(chunk, cpad)
    kern = functools.partial(_retention_kernel, seq=seq, chunk=chunk, cpad=cpad, n_heads=n_heads, n_seqs=n_seqs)
    head_tab = lambda shape: pl.BlockSpec((n_heads,) + shape, lambda b, h: (h, 0, 0))
    tok = lambda w: pl.BlockSpec((n_seqs * seq, n_heads * w), lambda b, h: (b, h))
    state = pl.BlockSpec((n_seqs, n_heads, RET_DK, RET_DV), lambda b, h: (b, h, 0, 0))
    return pl.pallas_call(
        kern,
        grid=(batch // n_seqs, RET_HEADS // n_heads),
        in_specs=[tok(RET_DK), tok(RET_DK), tok(RET_DV), tok(RET_DV), state,
                  head_tab((cpad, cpad)), head_tab((cpad, RET_DK)), head_tab((cpad, RET_DK)),
                  head_tab((1, RET_DV)),
                  pl.BlockSpec((1, n_heads * RET_DV), lambda b, h: (0, h))],
        out_specs=[tok(RET_DV), state],
        out_shape=[
            jax.ShapeDtypeStruct((batch * seq, RET_V_W), out_dtype),
            jax.ShapeDtypeStruct((batch, RET_HEADS, RET_DK, RET_DV), _F32),
        ],
        compiler_params=_params(2),
        name="retention",
    )(rq, rk, rv, rg, s0, intra, q_dec, k_dec, c_dec, gn_w)


def _split3_bf16(x):
    h = x.astype(_BF)
    r = x - h.astype(_F32)
    m = r.astype(_BF)
    l = (r - m.astype(_F32)).astype(_BF)
    return h, m, l


def _moba_prompt_kernel(q_ref, kt_ref, vt_ref, slope_ref, o_ref, *, seq):
    blk = MOBA_BLOCK
    nb = seq // blk
    hw = MOBA_DH
    scale = MOBA_DH ** -0.5
    q = q_ref[...] * (scale * LOG2E)
    k = kt_ref[0].T
    vt = jnp.concatenate([vt_ref[0].astype(_BF), jnp.ones((ONES_ROWS, seq), _BF)], axis=0)
    kmean = jnp.concatenate(
        [jnp.sum(k[n * blk:(n + 1) * blk], axis=0, keepdims=True) for n in range(nb)], axis=0) * (1.0 / blk)

    k_lane = lax.broadcasted_iota(jnp.int32, (seq, LANES), 1)
    k_off = (lax.broadcasted_iota(jnp.int32, (seq, LANES), 0) % blk).astype(_F32)
    k_aug = jnp.where(k_lane < 3, k_off, jnp.where(k_lane < 6, 1.0, 0.0))
    kb = jnp.concatenate([k.astype(_BF), k_aug.astype(_BF)], axis=1)

    q_lane = lax.broadcasted_iota(jnp.int32, (2 * blk, LANES), 1)
    q_row = lax.broadcasted_iota(jnp.int32, (2 * blk, LANES), 0)
    s_col = jnp.where(q_row < blk, slope_ref[0:1, 0:1], slope_ref[0:1, hw:hw + 1]) * LOG2E
    c_col = -s_col * (q_row % blk).astype(_F32)
    s3, c3 = _split3_bf16(s_col), _split3_bf16(c_col)
    q_aug = jnp.zeros((2 * blk, LANES), _F32)
    for j in range(3):
        q_aug = jnp.where(q_lane == j, s3[j].astype(_F32), jnp.where(q_lane == 3 + j, c3[j].astype(_F32), q_aug))
    q_aug = q_aug.astype(_BF)

    lane_head = lax.broadcasted_iota(jnp.int32, (blk, 2 * hw), 1) // hw
    key_i = lax.broadcasted_iota(jnp.int32, (blk, 2 * blk), 0)
    qry_i = lax.broadcasted_iota(jnp.int32, (blk, 2 * blk), 1) % blk
    causal = qry_i >= key_i
    col_head = lax.broadcasted_iota(jnp.int32, (1, 2 * blk), 1) // blk
    slope = jnp.where(col_head == 0, slope_ref[0:1, 0:1], slope_ref[0:1, hw:hw + 1]) * LOG2E
    row_id = lax.broadcasted_iota(jnp.int32, (nb, 2 * blk), 0)

    def scores(i):
        qi = q[i * blk:(i + 1) * blk]
        qexp = jnp.concatenate([jnp.where(lane_head == 0, qi, 0.0), jnp.where(lane_head == 1, qi, 0.0)], axis=0)
        st = _nt(kb[:(i + 1) * blk], jnp.concatenate([qexp.astype(_BF), q_aug], axis=1))
        return i, qexp, st

    def row_max(i, qexp, st):
        sel = None
        if i > MOBA_TOPK:
            gt = _nt3(kmean, qexp)
            valid = row_id < i
            sel = []
            for n in range(i):
                gn = gt[n:n + 1]
                ahead = ((gt > gn) | ((gt == gn) & (row_id < n))) & valid
                rank = jnp.sum(ahead.astype(_F32), axis=0, keepdims=True)
                sel.append(rank < MOBA_TOPK)
        keep = [causal if n == i else (None if sel is None else sel[n]) for n in range(i + 1)]
        shifts = [slope * float(-(i - n) * blk) for n in range(i + 1)]
        tops = []
        for n in range(i + 1):
            t = st[n * blk:(n + 1) * blk]
            if keep[n] is not None:
                t = jnp.where(keep[n], t, NEG_BIG)
            tops.append(jnp.max(t, axis=0, keepdims=True) + shifts[n])
        m = functools.reduce(jnp.maximum, tops)
        return i, st, keep, [sh - m for sh in shifts]

    def probs(i, st, keep, offs):
        ps = []
        for n in range(i + 1):
            t = st[n * blk:(n + 1) * blk] + offs[n]
            if keep[n] is not None:
                t = jnp.where(keep[n], t, NEG_BIG)
            ps.append(jnp.exp2(t).astype(_BF))
        return i, (ps[0] if i == 0 else jnp.concatenate(ps, axis=0))

    def finish(i, pt):
        ot = _nn(vt[:, :(i + 1) * blk], pt)
        ot = ot[:2 * hw] * (1.0 / ot[2 * hw:2 * hw + 1])
        oi = jnp.concatenate([ot[0:hw, 0:blk], ot[hw:2 * hw, blk:2 * blk]], axis=0)
        o_ref[i * blk:(i + 1) * blk, :] = oi.T.astype(o_ref.dtype)

    stages = (scores, row_max, probs, finish)
    in_flight = [None] * len(stages)
    for step in range(nb + len(stages) - 1):
        for j in reversed(range(1, len(stages))):
            if in_flight[j] is not None:
                out = stages[j](*in_flight[j])
                in_flight[j] = None
                if j + 1 < len(stages):
                    in_flight[j + 1] = out
        if step < nb:
            in_flight[1] = scores(step)


def _moba_prompt(mq, mkt, mvt, slope_row, batch, seq, out_dtype):
    width = 2 * MOBA_DH
    assert width == LANES and seq % MOBA_BLOCK == 0
    kern = functools.partial(_moba_prompt_kernel, seq=seq)
    spec = pl.BlockSpec((seq, width), lambda b, h: (b, h))
    spec_t = pl.BlockSpec((1, width, seq), lambda b, h: (b, h, 0))
    return pl.pallas_call(
        kern,
        grid=(batch, MOBA_W // width),
        in_specs=[spec, spec_t, spec_t, pl.BlockSpec((1, width), lambda b, h: (0, h))],
        out_specs=spec,
        out_shape=jax.ShapeDtypeStruct((batch * seq, MOBA_W), out_dtype),
        compiler_params=_params(2),
        name="moba_prompt",
    )(mq, mkt, mvt, slope_row)


RING_SLOTS = 4
KEY_GROUPS = 4


def _moba_sample_kernel(pt_ref, q_ref, kn_ref, vn_ref, slope_ref, ck_hbm, cv_hbm, *rest,
                        n_pages, page, t_new, with_proj):
    if with_proj:
        x_ref, w_ref, o_ref = rest[:3]
        proj_refs = rest[3:11]
        ring, s_scr, p_scr, bias_scr, sem = rest[11:]
        xb = x_ref[...].astype(_BF)
    else:
        o_ref, ring, s_scr, p_scr, bias_scr, sem = rest

    def proj_piece(c):
        if with_proj:
            for j in PROJ_PIECES[c]:
                _in_proj_segment(xb, w_ref, proj_refs, j)

    b = pl.program_id(0)
    n_seq = pl.num_programs(0)
    blk = MOBA_BLOCK
    past = n_pages * page
    nb = past // blk
    rows = MOBA_HEADS * t_new
    scale = MOBA_DH ** -0.5
    ppc = n_pages // KEY_GROUPS
    cw = ppc * page
    bpc = cw // blk
    n_chunks = 2 * KEY_GROUPS
    lead = RING_SLOTS - 1

    def copies(seq_idx, c):
        src = ck_hbm if c < KEY_GROUPS else cv_hbm
        first = (c % KEY_GROUPS) * ppc
        slot = c % RING_SLOTS
        return [pltpu.make_async_copy(src.at[pt_ref[seq_idx, first + j]],
                                      ring.at[slot, :, pl.ds(j * page, page)], sem.at[slot])
                for j in range(ppc)]

    def start(seq_idx, c):
        for d in copies(seq_idx, c):
            d.start()

    q_rows = jnp.concatenate([q_ref[...]] * MOBA_HEADS, axis=0)
    r_head = lax.broadcasted_iota(jnp.int32, (rows, MOBA_W), 0) // t_new
    c_head = lax.broadcasted_iota(jnp.int32, (rows, MOBA_W), 1) // MOBA_DH
    own = r_head == c_head
    q_hi, q_lo = _split_bf16(jnp.where(own, q_rows, 0.0) * scale)
    q_stack = jnp.concatenate([q_hi, q_lo], axis=0)
    slope = jnp.sum(jnp.where(own, jnp.broadcast_to(slope_ref[...], (rows, MOBA_W)), 0.0),
                    axis=-1, keepdims=True) * (1.0 / MOBA_DH)
    tok = lax.broadcasted_iota(jnp.int32, (rows, blk), 0) % t_new
    off = lax.broadcasted_iota(jnp.int32, (rows, blk), 1)

    @pl.when(b == 0)
    def _():
        for c in range(lead):
            start(0, c)
        for n in range(nb):
            dist = (past - n * blk) + (tok - off)
            bias_scr[:, n * blk:(n + 1) * blk] = -slope * dist.astype(_F32)

    gate = jnp.zeros((rows, nb), _F32)
    bid = lax.broadcasted_iota(jnp.int32, (rows, nb), 1)
    acc = den = None
    for c in range(n_chunks):
        nxt = c + lead
        if nxt < n_chunks:
            start(b, nxt)
        else:
            @pl.when(b + 1 < n_seq)
            def _():
                start(b + 1, nxt - n_chunks)
        proj_piece(c)
        for d in copies(b, c):
            d.wait()
        slot = c % RING_SLOTS
        if c < KEY_GROUPS:
            kc = ring[slot]
            k_hi, k_lo = _split_bf16(kc)
            r1 = _nn(q_stack, k_hi)
            s_scr[:, c * cw:(c + 1) * cw] = r1[:rows]
            g = r1[:rows] + (r1[rows:] + _nn(q_hi, k_lo))
            for j in range(bpc):
                gs = jnp.sum(g[:, j * blk:(j + 1) * blk], axis=-1, keepdims=True)
                gate = jnp.where(bid == c * bpc + j, gs, gate)
        else:
            cc = c - KEY_GROUPS
            acc = acc + _nt(p_scr[:, cc * cw:(cc + 1) * cw], ring[slot].astype(_BF))
        if c == KEY_GROUPS - 1:
            sel = jnp.zeros((rows, nb), jnp.bool_)
            for _ in range(min(MOBA_TOPK, nb)):
                mx = jnp.max(gate, axis=-1, keepdims=True)
                first = jnp.min(jnp.where(gate == mx, bid, nb), axis=-1, keepdims=True)
                hit = bid == first
                sel = sel | hit
                gate = jnp.where(hit, -jnp.inf, gate)
            zpad = jnp.zeros((page - t_new, MOBA_W), _F32)
            kn = jnp.concatenate([kn_ref[...], zpad], axis=0).astype(_BF)
            vn = jnp.concatenate([vn_ref[...], zpad], axis=0).astype(_BF)
            tok_p = lax.broadcasted_iota(jnp.int32, (rows, page), 0) % t_new
            off_p = lax.broadcasted_iota(jnp.int32, (rows, page), 1)
            s_own = _nt(q_hi, kn) - slope * (tok_p - off_p).astype(_F32)
            s_own = jnp.where(off_p <= tok_p, s_own, NEG_BIG)
            mrun = jnp.full((rows, blk), NEG_BIG, _F32)
            for n in range(nb):
                cols = slice(n * blk, (n + 1) * blk)
                t = jnp.where(sel[:, n:n + 1], s_scr[:, cols] + bias_scr[:, cols], NEG_BIG)
                s_scr[:, cols] = t
                mrun = jnp.maximum(mrun, t)
            m = jnp.maximum(jnp.max(mrun, axis=-1, keepdims=True), jnp.max(s_own, axis=-1, keepdims=True))
            p_own = jnp.exp(s_own - m)
            lrun = jnp.zeros((rows, blk), _F32)
            for n in range(nb):
                cols = slice(n * blk, (n + 1) * blk)
                e = jnp.exp(s_scr[:, cols] - m)
                lrun = lrun + e
                p_scr[:, cols] = e.astype(_BF)
            den = jnp.sum(lrun, axis=-1, keepdims=True) + jnp.sum(p_own, axis=-1, keepdims=True)
            acc = _nn(p_own.astype(_BF), vn)

    out = acc * (1.0 / den)
    o_ref[...] = jnp.concatenate(
        [out[h * t_new:(h + 1) * t_new, h * MOBA_DH:(h + 1) * MOBA_DH] for h in range(MOBA_HEADS)],
        axis=1).astype(o_ref.dtype)


def _moba_sample(mq, mk, mv, slope_row, cache_kt, cache_vt, page_table, t_new, out_dtype, proj=None):
    batch, n_pages = page_table.shape
    page = cache_kt.shape[2]
    assert page == LANES and t_new <= page and MOBA_BLOCK % page == 0
    assert n_pages % KEY_GROUPS == 0 and (n_pages // KEY_GROUPS * page) % MOBA_BLOCK == 0
    assert (2 * KEY_GROUPS) % RING_SLOTS == 0
    past = n_pages * page
    cw = n_pages // KEY_GROUPS * page
    rows = MOBA_HEADS * t_new
    kern = functools.partial(_moba_sample_kernel, n_pages=n_pages, page=page, t_new=t_new,
                             with_proj=proj is not None)
    tok = pl.BlockSpec((t_new, MOBA_W), lambda b, pt: (b, 0))
    in_specs = [tok, tok, tok, pl.BlockSpec((1, MOBA_W), lambda b, pt: (0, 0)),
                pl.BlockSpec(memory_space=pl.ANY), pl.BlockSpec(memory_space=pl.ANY)]
    out_specs, out_shape = tok, jax.ShapeDtypeStruct((batch * t_new, MOBA_W), out_dtype)
    operands = (page_table, mq, mk, mv, slope_row, cache_kt, cache_vt)
    if proj is not None:
        x2d, w_in_bf, inter_dtype, seq = proj
        tmx = x2d.shape[0] // batch
        assert x2d.shape[0] == tmx * batch
        p_specs, p_shape = _in_proj_outputs(x2d.shape[0], inter_dtype, tmx, seq)
        in_specs += [pl.BlockSpec((tmx, D_MODEL), lambda b, pt: (b, 0)), _const_spec((D_MODEL, N_IN))]
        out_specs, out_shape = [out_specs] + p_specs, [out_shape] + p_shape
        operands += (x2d, w_in_bf)
    grid_spec = pltpu.PrefetchScalarGridSpec(
        num_scalar_prefetch=1,
        grid=(batch,),
        in_specs=in_specs,
        out_specs=out_specs,
        scratch_shapes=[
            pltpu.VMEM((RING_SLOTS, MOBA_W, cw), _F32),
            pltpu.VMEM((rows, past), _F32),
            pltpu.VMEM((rows, past), _BF),
            pltpu.VMEM((rows, past), _F32),
            pltpu.SemaphoreType.DMA((RING_SLOTS,)),
        ],
    )
    res = pl.pallas_call(
        kern,
        grid_spec=grid_spec,
        out_shape=out_shape,
        compiler_params=_params(1),
        name="moba_sample",
    )(*operands)
    return res if proj is None else (res[0], tuple(res[1:]))


def _merge_kernel(x_ref, ro_ref, mo_ref, gab_ref, wr_ref, wm_ref, wo_ref, lnw_ref, lnb_ref, h_ref, *, alpha):
    a = _nn(ro_ref[...].astype(_BF), wr_ref[...])
    m = _nn(mo_ref[...].astype(_BF), wm_ref[...])
    ga = gab_ref[:, :D_MODEL].astype(_F32)
    gb = gab_ref[:, D_MODEL:].astype(_F32)
    merged = jax.nn.sigmoid(ga) * a + jax.nn.sigmoid(gb) * m
    pre = alpha * x_ref[...] + _nn(merged.astype(_BF), wo_ref[...])
    h_ref[...] = _layer_norm(pre, lnw_ref[...], lnb_ref[...])


def _merge(x2d, ro, mo, gab, w_ret_out, w_moba_out, w_o, ln_w, ln_b, alpha, tm):
    n = x2d.shape[0]
    assert n % tm == 0
    row = lambda w: pl.BlockSpec((tm, w), lambda i: (i, 0))
    return pl.pallas_call(
        functools.partial(_merge_kernel, alpha=alpha),
        grid=(n // tm,),
        in_specs=[row(D_MODEL), row(RET_V_W), row(MOBA_W), row(2 * D_MODEL),
                  _const_spec((RET_V_W, D_MODEL)), _const_spec((MOBA_W, D_MODEL)),
                  _const_spec((D_MODEL, D_MODEL)), _const_spec((1, D_MODEL)), _const_spec((1, D_MODEL))],
        out_specs=row(D_MODEL),
        out_shape=jax.ShapeDtypeStruct((n, D_MODEL), _F32),
        compiler_params=_params(1),
        name="merge",
    )(x2d, ro, mo, gab, w_ret_out, w_moba_out, w_o, ln_w, ln_b)


FFN_COL_CHUNK = D_FF
CONV_PAD = 8
FUSED_ROW_TILE = 512
LONG_ROW_TILE = 512
SHORT_ROW_TILE = 128
PROJ_RIDE_TILES = (128, 256, 512)
FUSED_SUBTILES = 2


def _ffn_kernel(*refs, alpha, tm, seq, within_seq):
    if within_seq:
        (h_ref, wup_ref, cw_ref, cb_ref, wdn_ref, lnw_ref, lnb_ref, y_ref, cs_ref, ubuf) = refs
    else:
        (h_ref, p1_ref, p2_ref, wup_ref, cw_ref, cb_ref, wdn_ref, lnw_ref, lnb_ref, y_ref, ua_ref, ubuf) = refs
    h = h_ref[...]
    hb = h.astype(_BF)
    hist = CONV_W - 1

    if within_seq:
        tiles_per_seq = seq // tm

        @pl.when(pl.program_id(0) % tiles_per_seq == 0)
        def _():
            ubuf[0:CONV_PAD, :] = jnp.zeros((CONV_PAD, D_FF), _F32)
    else:
        t_in_seq = lax.broadcasted_iota(jnp.int32, (tm, 1), 0) % seq

        @pl.when(pl.program_id(0) == 0)
        def _():
            ubuf[0:CONV_PAD, :] = jnp.zeros((CONV_PAD, D_FF), _F32)

    f = jnp.zeros((tm, D_MODEL), _F32)
    for c0 in range(0, D_FF, FFN_COL_CHUNK):
        cols = slice(c0, c0 + FFN_COL_CHUNK)
        ua = _nn(hb, wup_ref[:, c0:c0 + FFN_COL_CHUNK])
        ug = _nn(hb, wup_ref[:, D_FF + c0:D_FF + c0 + FFN_COL_CHUNK])
        ubuf[CONV_PAD:CONV_PAD + tm, cols] = ua
        sh1 = ubuf[CONV_PAD - 1:CONV_PAD - 1 + tm, cols]
        sh2 = ubuf[CONV_PAD - 2:CONV_PAD - 2 + tm, cols]
        if not within_seq:
            sh1 = jnp.where(t_in_seq >= 1, sh1, p1_ref[:, cols])
            sh2 = jnp.where(t_in_seq >= 2, sh2, p2_ref[:, cols])
            ua_ref[:, cols] = ua
        uc = cb_ref[:, cols] + (sh2 * cw_ref[0:1, cols] + sh1 * cw_ref[1:2, cols] + ua * cw_ref[2:3, cols])
        act = (uc * jax.nn.sigmoid(uc) * ug).astype(_BF)
        f = f + _nn(act, wdn_ref[c0:c0 + FFN_COL_CHUNK, :])

    if within_seq:
        last = ubuf[CONV_PAD + tm - hist:CONV_PAD + tm, :]
        cs_ref[0] = last
        ubuf[CONV_PAD - hist:CONV_PAD, :] = last
    y_ref[...] = _layer_norm(alpha * h + f, lnw_ref[...], lnb_ref[...])


def _merge_ffn_kernel(x_ref, ro_ref, mo_ref, gab_ref, wr_ref, wm_ref, wo_ref, l1w_ref, l1b_ref,
                      wup_ref, cw_ref, cb_ref, wdn_ref, l2w_ref, l2b_ref, y_ref, cs_ref, ubuf, *, alpha, tm, seq):
    hist = CONV_W - 1
    sub = tm // FUSED_SUBTILES

    @pl.when(pl.program_id(0) % (seq // tm) == 0)
    def _():
        ubuf[0:CONV_PAD, :] = jnp.zeros((CONV_PAD, D_FF), _F32)

    def mix_mm(r):
        rows = slice(r * sub, (r + 1) * sub)
        return r, _nn(ro_ref[rows, :].astype(_BF), wr_ref[...]), _nn(mo_ref[rows, :].astype(_BF), wm_ref[...])

    def gate(r, a, m):
        rows = slice(r * sub, (r + 1) * sub)
        ga = gab_ref[rows, :D_MODEL].astype(_F32)
        gb = gab_ref[rows, D_MODEL:].astype(_F32)
        return r, (jax.nn.sigmoid(ga) * a + jax.nn.sigmoid(gb) * m).astype(_BF)

    def out_mm(r, merged):
        return r, _nn(merged, wo_ref[...])

    def norm1(r, proj):
        rows = slice(r * sub, (r + 1) * sub)
        h = _layer_norm(alpha * x_ref[rows, :] + proj, l1w_ref[...], l1b_ref[...])
        return r, h, h.astype(_BF)

    def up_mm(r, h, hb):
        ua = _nn(hb, wup_ref[:, :D_FF])
        ug = _nn(hb, wup_ref[:, D_FF:])
        ubuf[CONV_PAD + r * sub:CONV_PAD + (r + 1) * sub, :] = ua
        return r, h, ua, ug

    def conv_act(r, h, ua, ug):
        lo = CONV_PAD + r * sub
        sh1 = ubuf[lo - 1:lo - 1 + sub, :]
        sh2 = ubuf[lo - 2:lo - 2 + sub, :]
        uc = cb_ref[...] + (sh2 * cw_ref[0:1, :] + sh1 * cw_ref[1:2, :] + ua * cw_ref[2:3, :])
        return r, h, (uc * jax.nn.sigmoid(uc) * ug).astype(_BF)

    def down_mm(r, h, act):
        return r, h, _nn(act, wdn_ref[...])

    def norm2(r, h, f):
        y_ref[r * sub:(r + 1) * sub, :] = _layer_norm(alpha * h + f, l2w_ref[...], l2b_ref[...])

    stages = (mix_mm, gate, out_mm, norm1, up_mm, conv_act, down_mm, norm2)
    in_flight = [None] * len(stages)
    for step in range(FUSED_SUBTILES + len(stages) - 1):
        for j in reversed(range(1, len(stages))):
            if in_flight[j] is not None:
                out = stages[j](*in_flight[j])
                in_flight[j] = None
                if j + 1 < len(stages):
                    in_flight[j + 1] = out
        if step < FUSED_SUBTILES:
            in_flight[1] = mix_mm(step)

    last = ubuf[CONV_PAD + tm - hist:CONV_PAD + tm, :]
    cs_ref[0] = last
    ubuf[CONV_PAD - hist:CONV_PAD, :] = last


def _merge_ffn_prompt(x2d, ro, mo, gab, wts, alpha, batch, seq, tm):
    (_, _, w_ret_out, w_moba_out, w_o, ln1_w, ln1_b, w_up, conv_w, conv_b, w_down, ln2_w, ln2_b) = wts
    n = x2d.shape[0]
    assert seq % tm == 0 and tm >= CONV_W - 1
    tiles_per_seq = seq // tm
    row = lambda w: pl.BlockSpec((tm, w), lambda i: (i, 0))
    return pl.pallas_call(
        functools.partial(_merge_ffn_kernel, alpha=alpha, tm=tm, seq=seq),
        grid=(n // tm,),
        in_specs=[row(D_MODEL), row(RET_V_W), row(MOBA_W), row(2 * D_MODEL),
                  _const_spec((RET_V_W, D_MODEL)), _const_spec((MOBA_W, D_MODEL)), _const_spec((D_MODEL, D_MODEL)),
                  _const_spec((1, D_MODEL)), _const_spec((1, D_MODEL)),
                  _const_spec((D_MODEL, 2 * D_FF)), _const_spec((CONV_W, D_FF)), _const_spec((1, D_FF)),
                  _const_spec((D_FF, D_MODEL)), _const_spec((1, D_MODEL)), _const_spec((1, D_MODEL))],
        out_specs=[row(D_MODEL), pl.BlockSpec((1, CONV_W - 1, D_FF), lambda i: (i // tiles_per_seq, 0, 0))],
        out_shape=[jax.ShapeDtypeStruct((n, D_MODEL), _F32),
                   jax.ShapeDtypeStruct((batch, CONV_W - 1, D_FF), _F32)],
        scratch_shapes=[pltpu.VMEM((CONV_PAD + tm, D_FF), _F32)],
        compiler_params=_params(1),
        name="merge_ffn",
    )(x2d, ro, mo, gab, w_ret_out, w_moba_out, w_o, ln1_w, ln1_b, w_up, conv_w, conv_b, w_down, ln2_w, ln2_b)


def _ffn_prompt(h2d, w_up, conv_w, conv_b, w_down, ln_w, ln_b, alpha, batch, seq, tm):
    n = h2d.shape[0]
    assert seq % tm == 0 and tm >= CONV_W - 1
    tiles_per_seq = seq // tm
    row = pl.BlockSpec((tm, D_MODEL), lambda i: (i, 0))
    return pl.pallas_call(
        functools.partial(_ffn_kernel, alpha=alpha, tm=tm, seq=seq, within_seq=True),
        grid=(n // tm,),
        in_specs=[row, _const_spec((D_MODEL, 2 * D_FF)), _const_spec((CONV_W, D_FF)), _const_spec((1, D_FF)),
                  _const_spec((D_FF, D_MODEL)), _const_spec((1, D_MODEL)), _const_spec((1, D_MODEL))],
        out_specs=[row, pl.BlockSpec((1, CONV_W - 1, D_FF), lambda i: (i // tiles_per_seq, 0, 0))],
        out_shape=[jax.ShapeDtypeStruct((n, D_MODEL), _F32),
                   jax.ShapeDtypeStruct((batch, CONV_W - 1, D_FF), _F32)],
        scratch_shapes=[pltpu.VMEM((CONV_PAD + tm, D_FF), _F32)],
        compiler_params=_params(1),
        name="ffn_prompt",
    )(h2d, w_up, conv_w, conv_b, w_down, ln_w, ln_b)


def _ffn_sample(h2d, prev1, prev2, w_up, conv_w, conv_b, w_down, ln_w, ln_b, alpha, seq, tm):
    n = h2d.shape[0]
    assert n % tm == 0 and tm % seq == 0
    row = pl.BlockSpec((tm, D_MODEL), lambda i: (i, 0))
    wide = pl.BlockSpec((tm, D_FF), lambda i: (i, 0))
    return pl.pallas_call(
        functools.partial(_ffn_kernel, alpha=alpha, tm=tm, seq=seq, within_seq=False),
        grid=(n // tm,),
        in_specs=[row, wide, wide, _const_spec((D_MODEL, 2 * D_FF)), _const_spec((CONV_W, D_FF)),
                  _const_spec((1, D_FF)), _const_spec((D_FF, D_MODEL)), _const_spec((1, D_MODEL)),
                  _const_spec((1, D_MODEL))],
        out_specs=[row, wide],
        out_shape=[jax.ShapeDtypeStruct((n, D_MODEL), _F32), jax.ShapeDtypeStruct((n, D_FF), _F32)],
        scratch_shapes=[pltpu.VMEM((CONV_PAD + tm, D_FF), _F32)],
        compiler_params=_params(1),
        name="ffn_sample",
    )(h2d, prev1, prev2, w_up, conv_w, conv_b, w_down, ln_w, ln_b)


def _pick_tile(n, cap):
    t = cap
    while n % t:
        t //= 2
    return t


def _layer(x, proj, ret_state, conv_state, ret_chunk, ret_tiling, attend, wts, alpha, inter_dtype, row_tile):
    (w_in, gn_w, w_ret_out, w_moba_out, w_o, ln1_w, ln1_b, w_up, conv_w, conv_b, w_down, ln2_w, ln2_b) = wts
    batch, seq, _ = x.shape
    n = batch * seq
    x2d = x.reshape(n, D_MODEL)
    tm = _pick_tile(n, row_tile)
    time_minor = conv_state is None
    rq, rk, rv, rg, mq, mk, mv, gab = proj
    ro, ret_new = _retention(rq, rk, rv, rg, ret_state, gn_w, batch, seq, ret_chunk, inter_dtype, *ret_tiling)
    mo = attend(mq, mk, mv)
    if conv_state is None:
        y, conv_new = _merge_ffn_prompt(x2d, ro, mo, gab, wts, alpha, batch, seq, _pick_tile(seq, FUSED_ROW_TILE))
    else:
        h = _merge(x2d, ro, mo, gab, w_ret_out, w_moba_out, w_o, ln1_w, ln1_b, alpha, tm)
        hist = CONV_W - 1
        t_idx = jnp.arange(seq)[None, :, None]
        newest, older = conv_state[:, hist - 1][:, None, :], conv_state[:, hist - 2][:, None, :]
        prev1 = jnp.where(t_idx == 0, newest, 0.0)
        prev2 = jnp.where(t_idx == 0, older, jnp.where(t_idx == 1, newest, 0.0))
        y, ua = _ffn_sample(h, prev1.reshape(n, D_FF), prev2.reshape(n, D_FF), w_up, conv_w, conv_b, w_down,
                            ln2_w, ln2_b, alpha, seq, tm)
        conv_new = jnp.concatenate([conv_state, ua.reshape(batch, seq, D_FF)], axis=1)[:, -hist:]
    if time_minor:
        unpack = lambda a: a.reshape(batch, MOBA_HEADS, MOBA_DH, seq).transpose(0, 3, 1, 2)
    else:
        unpack = lambda a: a.reshape(batch, seq, MOBA_HEADS, MOBA_DH)
    return y.reshape(batch, seq, D_MODEL), ret_new, conv_new, unpack(mk), unpack(mv)


def kernel(x_prompt, x_sample, cache_k, cache_v, page_table, state_ret, state_conv, w_in, ret_gn_w, w_ret_out,
           w_moba_out, w_o, ln1_w, ln1_b, w_up, conv_w, conv_b, w_down, ln2_w, ln2_b):
    depth = w_in.shape[0]
    alpha = (2 * depth) ** 0.25
    batch, seq, _ = x_prompt.shape
    dec_batch, dec_seq, _ = x_sample.shape
    n_phys, page = cache_k.shape[1], cache_k.shape[2]
    slope_row = jnp.repeat(jnp.exp2(-8.0 * (jnp.arange(MOBA_HEADS, dtype=_F32) + 1.0) / MOBA_HEADS),
                           MOBA_DH)[None, :]
    hp, hs = x_prompt, x_sample
    outs = [[] for _ in range(8)]
    for l in range(depth):
        wts = (w_in[l].astype(_BF), ret_gn_w[l][None, :], w_ret_out[l].astype(_BF), w_moba_out[l].astype(_BF),
               w_o[l].astype(_BF), ln1_w[l][None, :], ln1_b[l][None, :], w_up[l].astype(_BF), conv_w[l],
               conv_b[l][None, :], w_down[l].astype(_BF), ln2_w[l][None, :], ln2_b[l][None, :])
        xp2d = hp.reshape(batch * seq, D_MODEL)
        xs2d = hs.reshape(dec_batch * dec_seq, D_MODEL)
        ck = jnp.transpose(cache_k[l], (0, 2, 3, 1)).reshape(n_phys, MOBA_W, page)
        cv = jnp.transpose(cache_v[l], (0, 2, 3, 1)).reshape(n_phys, MOBA_W, page)
        proj_s = _in_proj(xs2d, wts[0], _F32, _pick_tile(dec_batch * dec_seq, SHORT_ROW_TILE))
        moba_s = functools.partial(_moba_sample, proj_s[4], proj_s[5], proj_s[6], slope_row, ck, cv, page_table,
                                   dec_seq, _F32)
        tile_p = (batch * seq) // dec_batch
        if tile_p * dec_batch == batch * seq and tile_p in PROJ_RIDE_TILES and seq % tile_p == 0:
            mo_s, proj_p = moba_s(proj=(xp2d, wts[0], _BF, seq))
        else:
            mo_s = moba_s()
            proj_p = _in_proj(xp2d, wts[0], _BF, _pick_tile(batch * seq, LONG_ROW_TILE), seq)
        attend_p = functools.partial(_moba_prompt, slope_row=slope_row, batch=batch, seq=seq, out_dtype=_BF)
        hp, rp, cp, kp, vp = _layer(hp, proj_p, jnp.zeros((batch, RET_HEADS, RET_DK, RET_DV), _F32), None,
                                    RET_CHUNK, (1, 1), attend_p, wts, alpha, _BF, LONG_ROW_TILE)
        hs, rs, cs, ks_, vs_ = _layer(hs, proj_s, state_ret[l], state_conv[l], dec_seq,
                                      (RET_HEADS, _pick_tile(dec_batch, 8)), lambda *_: mo_s, wts, alpha, _F32,
                                      SHORT_ROW_TILE)
        for lst, val in zip(outs, (kp, vp, rp, cp, ks_, vs_, rs, cs)):
            lst.append(val)
    return (hp, hs) + tuple(jnp.stack(o) for o in outs)
```
